```python
import jax, jax.numpy as jnp
from jax import lax
import numpy as np

D_MODEL = 1024
BATCH = 8
SEQ = 4096
DEPTH = 4

GRID_W = 64
CTX_LEN = 256
D_FF = 2816
MIX_W = D_MODEL
A_W = MIX_W // 2
B_W = MIX_W - A_W
GMLP_GROUPS = 4
GMLP_CH = A_W // GMLP_GROUPS
GMLP_CHUNK = 128
HGRN_HEADS = 4
HGRN_K = B_W // HGRN_HEADS
HGRN_V = B_W // HGRN_HEADS
HGRN_CHUNK = 64
IN_W = 2 * A_W + 5 * B_W
N_MOD = 9
EPS = 1e-6
POS_THETA = 10000.0

kernel_name = "hybrid_gmlp_hgrn2_diffusion_trunk"


def _rmsnorm(x, g):
    xf = x.astype(jnp.float32)
    y = xf * lax.rsqrt(jnp.mean(xf * xf, axis=-1, keepdims=True) + EPS)
    return (y * g.astype(jnp.float32)).astype(x.dtype)


def _modulation(cond, w, b):
    m = jax.nn.silu(cond) @ w + b
    return [m[:, None, j * D_MODEL:(j + 1) * D_MODEL] for j in range(N_MOD)]


def _modulate(x, g, shift, scale):
    return _rmsnorm(x, g) * (1 + scale) + shift


def _swiglu(h, w1, w3, w2):
    return (jax.nn.silu(h @ w1) * (h @ w3)) @ w2


def _grid_pos_embed(rows, dtype):
    row = jnp.repeat(jnp.arange(rows), GRID_W).astype(jnp.float32)
    col = jnp.tile(jnp.arange(GRID_W), rows).astype(jnp.float32)
    quarter = D_MODEL // 4
    freq = POS_THETA ** (-jnp.arange(quarter, dtype=jnp.float32) / quarter)

    def axis_embed(p):
        a = p[:, None] * freq[None, :]
        return jnp.concatenate([jnp.sin(a), jnp.cos(a)], axis=-1)

    return jnp.concatenate([axis_embed(row), axis_embed(col)], axis=-1).astype(dtype)


def _gmlp_spatial(u, v, w_s, b_s, g_v):
    B, L, _ = u.shape
    n = L // GMLP_CHUNK
    u = jax.nn.gelu(u).reshape(B, n, GMLP_CHUNK, GMLP_GROUPS, GMLP_CH)
    v = _rmsnorm(jax.nn.gelu(v).reshape(B, L, GMLP_GROUPS, GMLP_CH), g_v.reshape(GMLP_GROUPS, GMLP_CH))
    v = v.reshape(B, n, GMLP_CHUNK, GMLP_GROUPS, GMLP_CH)
    sv = jnp.einsum("gts,bnsgc->bntgc", w_s, v) + b_s.T[:, :, None]
    return (u * sv).reshape(B, L, A_W)


def _lower_bounds(p):
    cum = jnp.cumsum(jax.nn.softmax(p.astype(jnp.float32), axis=0), axis=0)
    return cum - cum[0:1]


def _forget(f_raw, lb):
    f_raw = f_raw.astype(jnp.float32)
    log_f = jnp.logaddexp(jnp.log(lb), jnp.log1p(-lb) + jax.nn.log_sigmoid(f_raw))
    k = (1.0 - lb) * jax.nn.sigmoid(-f_raw)
    return k, log_f


def _gla_scan(q, k, log_f, v, s0):
    B, L, H, K = q.shape
    V = v.shape[-1]
    n = L // HGRN_CHUNK

    def to_chunks(a):
        return a.reshape(B, n, HGRN_CHUNK, H, a.shape[-1]).transpose(1, 0, 3, 2, 4)

    mask = jnp.tril(jnp.ones((HGRN_CHUNK, HGRN_CHUNK), dtype=bool))[:, :, None]

    def step(S, inp):
        qc, kc, gc, vc = inp
        b = jnp.cumsum(gc, axis=2)
        diff = b[:, :, :, None, :] - b[:, :, None, :, :]
        decay = jnp.exp(jnp.where(mask, diff, -jnp.inf))
        scores = jnp.einsum("bhtk,bhsk,bhtsk->bhts", qc, kc, decay)
        o = (jnp.einsum("bhts,bhsv->bhtv", scores, vc)
             + jnp.einsum("bhtk,bhkv->bhtv", qc * jnp.exp(b), S))
        b_end = b[:, :, -1, :]
        S = (jnp.exp(b_end)[..., None] * S
             + jnp.einsum("bhsk,bhsv->bhkv", kc * jnp.exp(b_end[:, :, None, :] - b), vc))
        return S, o

    s_fin, o = lax.scan(step, s0, (to_chunks(q), to_chunks(k), to_chunks(log_f), to_chunks(v)))
    return o.transpose(1, 0, 3, 2, 4).reshape(B, L, H, V), s_fin


def _hgrn_bidir(q, f_fw, f_bw, i, lb_f, lb_b, s0_f, s0_b):
    B, L, _ = q.shape

    def heads(a):
        return a.astype(jnp.float32).reshape(B, L, HGRN_HEADS, -1)

    def flip(a):
        return a[:, ::-1]

    qh = heads(jax.nn.silu(q))
    vh = heads(i)
    kf, gf = _forget(f_fw, lb_f)
    kb, gb = _forget(f_bw, lb_b)
    o_f, s_f = _gla_scan(qh, heads(kf), heads(gf), vh, s0_f)
    o_b, s_b = _gla_scan(flip(qh), flip(heads(kb)), flip(heads(gb)), flip(vh), s0_b)
    return o_f + flip(o_b), s_f, s_b


def _token_mixer(h, w_in, w_out, w_s, b_s, g_v, g_o, lb_f, lb_b, s0_f, s0_b):
    B, L, _ = h.shape
    z = h @ w_in
    cuts = [A_W, 2 * A_W, 2 * A_W + B_W, 2 * A_W + 2 * B_W, 2 * A_W + 3 * B_W, 2 * A_W + 4 * B_W]
    u, v, q, f_fw, f_bw, i, og = jnp.split(z, cuts, axis=-1)
    a_out = _gmlp_spatial(u, v, w_s, b_s, g_v)
    o, s_f, s_b = _hgrn_bidir(q, f_fw, f_bw, i, lb_f, lb_b, s0_f, s0_b)
    o = _rmsnorm(o, g_o.reshape(HGRN_HEADS, HGRN_V)).reshape(B, L, B_W).astype(h.dtype)
    b_out = o * jax.nn.silu(og)
    return jnp.concatenate([a_out, b_out], axis=-1) @ w_out, s_f, s_b


def _context_states(h, w_in, lb_f, lb_b, s0_f, s0_b):
    z = h @ w_in[:, 2 * A_W:2 * A_W + 4 * B_W]
    q, f_fw, f_bw, i = jnp.split(z, 4, axis=-1)
    _, s_f, s_b = _hgrn_bidir(q, f_fw, f_bw, i, lb_f, lb_b, s0_f, s0_b)
    return s_f, s_b


def setup_inputs(seed: int = 0) -> dict:
    key = jax.random.key(seed)
    ks = jax.random.split(key, 24)

    def nrm(k, shape, s):
        return jax.random.normal(k, shape, jnp.float32) * s

    def gain(k, shape):
        return 1.0 + 0.02 * jax.random.normal(k, shape, jnp.float32)

    d_s = D_MODEL ** -0.5
    return {
        "x": nrm(ks[0], (BATCH, SEQ, D_MODEL), 1.0),
        "c": nrm(ks[1], (BATCH, D_MODEL), 1.0),
        "ctx": nrm(ks[2], (BATCH, CTX_LEN, D_MODEL), 1.0),
        "c_ctx": nrm(ks[3], (D_MODEL,), 1.0),
        "w_ada": nrm(ks[4], (DEPTH, D_MODEL, N_MOD * D_MODEL), d_s),
        "b_ada": nrm(ks[5], (DEPTH, N_MOD * D_MODEL), 0.02),
        "norm_ffn1_g": gain(ks[6], (DEPTH, D_MODEL)),
        "norm_mix_g": gain(ks[7], (DEPTH, D_MODEL)),
        "norm_ffn2_g": gain(ks[8], (DEPTH, D_MODEL)),
        "ffn1_w1": nrm(ks[9], (DEPTH, D_MODEL, D_FF), d_s),
        "ffn1_w3": nrm(ks[10], (DEPTH, D_MODEL, D_FF), d_s),
        "ffn1_w2": nrm(ks[11], (DEPTH, D_FF, D_MODEL), D_FF ** -0.5),
        "ffn2_w1": nrm(ks[12], (DEPTH, D_MODEL, D_FF), d_s),
        "ffn2_w3": nrm(ks[13], (DEPTH, D_MODEL, D_FF), d_s),
        "ffn2_w2": nrm(ks[14], (DEPTH, D_FF, D_MODEL), D_FF ** -0.5),
        "w_in": nrm(ks[15], (DEPTH, D_MODEL, IN_W), d_s),
        "w_out": nrm(ks[16], (DEPTH, MIX_W, D_MODEL), MIX_W ** -0.5),
        "gmlp_ws": nrm(ks[17], (DEPTH, GMLP_GROUPS, GMLP_CHUNK, GMLP_CHUNK), GMLP_CHUNK ** -0.5),
        "gmlp_bs": gain(ks[18], (DEPTH, GMLP_GROUPS, GMLP_CHUNK)),
        "gmlp_norm_g": gain(ks[19], (DEPTH, A_W)),
        "hgrn_lb_fwd": nrm(ks[20], (DEPTH, B_W), 0.5),
        "hgrn_lb_bwd": nrm(ks[21], (DEPTH, B_W), 0.5),
        "hgrn_norm_g": gain(ks[22], (DEPTH, B_W)),
        "norm_final_g": gain(ks[23], (D_MODEL,)),
    }


def reference(x, c, ctx, c_ctx, w_ada, b_ada, norm_ffn1_g, norm_mix_g, norm_ffn2_g,
              ffn1_w1, ffn1_w3, ffn1_w2, ffn2_w1, ffn2_w3, ffn2_w2, w_in, w_out,
              gmlp_ws, gmlp_bs, gmlp_norm_g, hgrn_lb_fwd, hgrn_lb_bwd, hgrn_norm_g, norm_final_g):
    B, L, _ = x.shape
    rows = L // GRID_W
    lat = x + _grid_pos_embed(rows, x.dtype)[None]
    cx = ctx
    lbs_f = _lower_bounds(hgrn_lb_fwd)
    lbs_b = _lower_bounds(hgrn_lb_bwd)
    zero_state = jnp.zeros((B, HGRN_HEADS, HGRN_K, HGRN_V), jnp.float32)

    for l in range(DEPTH):
        last = l == DEPTH - 1
        ml = _modulation(c, w_ada[l], b_ada[l])
        mc = _modulation(c_ctx[None], w_ada[l], b_ada[l])

        ffn1 = (ffn1_w1[l], ffn1_w3[l], ffn1_w2[l])
        lat = lat + 0.5 * ml[2] * _swiglu(_modulate(lat, norm_ffn1_g[l], ml[0], ml[1]), *ffn1)
        cx = cx + 0.5 * mc[2] * _swiglu(_modulate(cx, norm_ffn1_g[l], mc[0], mc[1]), *ffn1)

        mix_w = (w_in[l], w_out[l], gmlp_ws[l], gmlp_bs[l], gmlp_norm_g[l], hgrn_norm_g[l],
                 lbs_f[l], lbs_b[l])
        h_cx = _modulate(cx, norm_mix_g[l], mc[3], mc[4])
        if last:
            s_f, s_b = _context_states(h_cx, w_in[l], lbs_f[l], lbs_b[l], zero_state, zero_state)
        else:
            out_cx, s_f, s_b = _token_mixer(h_cx, *mix_w, zero_state, zero_state)
            cx = cx + mc[5] * out_cx
        out_lat, _, _ = _token_mixer(_modulate(lat, norm_mix_g[l], ml[3], ml[4]), *mix_w, s_f, s_b)
        lat = lat + ml[5] * out_lat

        ffn2 = (ffn2_w1[l], ffn2_w3[l], ffn2_w2[l])
        lat = lat + 0.5 * ml[8] * _swiglu(_modulate(lat, norm_ffn2_g[l], ml[6], ml[7]), *ffn2)
        if not last:
            cx = cx + 0.5 * mc[8] * _swiglu(_modulate(cx, norm_ffn2_g[l], mc[6], mc[7]), *ffn2)

    return _rmsnorm(lat, norm_final_g)
```

```python
import functools

import jax
import jax.numpy as jnp
from jax import lax
from jax.experimental import pallas as pl
from jax.experimental.pallas import tpu as pltpu

D_MODEL = 1024
D_FF = 2816
DEPTH = 4
GRID_W = 64
A_W = 512
B_W = 512
IN_W = 2 * A_W + 5 * B_W
N_GROUPS = 4
GROUP_W = 128
GMLP_CHUNK = 128
N_HEADS = 4
N_MOD = 9
MOD_ROWS = 16
EPS = 1e-6
POS_THETA = 10000.0

GLA_CHUNK = 64
GLA_BLOCK = 16
LOG_F_MIN = -10.0
FF_SPLITS = ((0, 1536), (1536, 1280))

V7X_VMEM_BYTES = 64 * 1024 * 1024
VMEM_LIMIT = V7X_VMEM_BYTES - 8 * 1024 * 1024

F32 = jnp.float32
BF16 = jnp.bfloat16


def _sigmoid(x):
    return 1.0 / (1.0 + jnp.exp(-x))


def _silu(x):
    return x * _sigmoid(x)


def _gelu_tanh(x):
    return x * (0.5 * (1.0 + jnp.tanh(0.7978845608028654 * (x + 0.044715 * (x * x * x)))))


def _rms(x, g):
    return x * lax.rsqrt(jnp.mean(x * x, axis=-1, keepdims=True) + EPS) * g


def _dot(a, b):
    return jnp.dot(a, b, preferred_element_type=F32)


def _dot_nt(a, b):
    return lax.dot_general(a, b, (((1,), (1,)), ((), ())), preferred_element_type=F32)


def _dot_tn(a, b):
    return lax.dot_general(a, b, (((0,), (0,)), ((), ())), preferred_element_type=F32)


def _const_spec(shape):
    nd = len(shape)
    return pl.BlockSpec(shape, lambda *_: (0,) * nd, pipeline_mode=pl.Buffered(1))


def _mod_kernel(cond_ref, w_ref, b_ref, o_ref):
    s = _silu(cond_ref[...])
    o_ref[...] = jnp.dot(s, w_ref[...], preferred_element_type=F32,
                         precision=lax.Precision.HIGHEST) + b_ref[...]


def _modulation(cond, w_ada, b_ada):
    n_out = N_MOD * D_MODEL
    return pl.pallas_call(
        _mod_kernel,
        grid=(DEPTH, N_MOD),
        in_specs=[
            pl.BlockSpec((MOD_ROWS, D_MODEL), lambda l, j: (0, 0)),
            pl.BlockSpec((None, D_MODEL, D_MODEL), lambda l, j: (l, 0, j)),
            pl.BlockSpec((None, 1, D_MODEL), lambda l, j: (l, 0, j)),
        ],
        out_specs=pl.BlockSpec((None, MOD_ROWS, D_MODEL), lambda l, j: (l, 0, j)),
        out_shape=jax.ShapeDtypeStruct((DEPTH, MOD_ROWS, n_out), F32),
        compiler_params=pltpu.CompilerParams(dimension_semantics=("arbitrary", "arbitrary")),
        name="adaln_modulation",
    )(cond, w_ada, b_ada.reshape(DEPTH, 1, n_out))


def _ffn_kernel(*refs, j0, add_pos, final):
    refs = list(refs)
    x_ref = refs.pop(0)
    pos_ref = refs.pop(0) if add_pos else None
    mod_ref, g_ref, w1_ref, w3_ref, w2_ref = refs[:5]
    refs = refs[5:]
    gfin_ref = refs.pop(0) if final else None
    o_ref = refs.pop(0)

    x = x_ref[...]
    if add_pos:
        x = x + pos_ref[...]
    shift = mod_ref[j0:j0 + 1, :]
    scale = mod_ref[j0 + 1:j0 + 2, :]
    gate = mod_ref[j0 + 2:j0 + 3, :]
    h = (_rms(x, g_ref[...]) * (1.0 + scale) + shift).astype(BF16)
    y = None
    for s, n in FF_SPLITS:
        a = _dot(h, w1_ref[:, s:s + n])
        b = _dot(h, w3_ref[:, s:s + n])
        p = (_silu(a) * b).astype(BF16)
        yc = _dot(p, w2_ref[s:s + n, :])
        y = yc if y is None else y + yc
    out = x + (0.5 * gate) * y
    if final:
        out = _rms(out, gfin_ref[...])
    o_ref[...] = out


def _ffn(x, mod, mod_row, g, w1, w3, w2, *, j0, tm, pos=None, gfin=None):
    B, L, _ = x.shape
    add_pos, final = pos is not None, gfin is not None
    tok = pl.BlockSpec((None, tm, D_MODEL), lambda b, t: (b, t, 0))
    in_specs, args = [tok], [x]
    if add_pos:
        in_specs.append(pl.BlockSpec((tm, D_MODEL), lambda b, t: (t, 0)))
        args.append(pos)
    in_specs += [
        pl.BlockSpec((None, N_MOD, D_MODEL), lambda b, t: (mod_row(b), 0, 0)),
        _const_spec((1, D_MODEL)),
        _const_spec((D_MODEL, D_FF)),
        _const_spec((D_MODEL, D_FF)),
        _const_spec((D_FF, D_MODEL)),
    ]
    args += [mod, g.reshape(1, D_MODEL), w1, w3, w2]
    if final:
        in_specs.append(_const_spec((1, D_MODEL)))
        args.append(gfin.reshape(1, D_MODEL))
    return pl.pallas_call(
        functools.partial(_ffn_kernel, j0=j0, add_pos=add_pos, final=final),
        grid=(B, L // tm),
        in_specs=in_specs,
        out_specs=tok,
        out_shape=jax.ShapeDtypeStruct(x.shape, F32),
        compiler_params=pltpu.CompilerParams(
            dimension_semantics=("parallel", "parallel"), vmem_limit_bytes=VMEM_LIMIT),
        name="adaln_swiglu",
    )(*args)


def _forget(f_raw, lb):
    e = jnp.exp(-jnp.abs(f_raw))
    inv = 1.0 / (1.0 + e)
    pos = f_raw >= 0.0
    sig_p = jnp.where(pos, inv, e * inv)
    sig_n = jnp.where(pos, e * inv, inv)
    k = (1.0 - lb) * sig_n
    log_f = jnp.maximum(jnp.log(lb + (1.0 - lb) * sig_p), LOG_F_MIN)
    return k, log_f


def _cumsum_rows(g, tri):
    hi = g.astype(BF16)
    r1 = g - hi.astype(F32)
    mid = r1.astype(BF16)
    lo = (r1 - mid.astype(F32)).astype(BF16)
    return _dot(tri, hi) + _dot(tri, mid) + _dot(tri, lo)


def _ref_rows(a, first, period):
    n = a.shape[0] // period
    parts = [jnp.broadcast_to(a[first + j * period:first + j * period + 1, :], (period, a.shape[1]))
             for j in range(n)]
    return parts[0] if n == 1 else jnp.concatenate(parts, axis=0)


def _gla_levels():
    levels = []
    m = GLA_BLOCK
    while m < GLA_CHUNK:
        levels.append(m)
        m *= 2
    return tuple(levels)


def _gla_masks():
    C = GLA_CHUNK
    t = lax.broadcasted_iota(jnp.int32, (C, C), 0)
    s = lax.broadcasted_iota(jnp.int32, (C, C), 1)
    blk = lambda i, m: lax.shift_right_logical(i, m.bit_length() - 1)
    same = blk(t, GLA_BLOCK) == blk(s, GLA_BLOCK)
    fwd = [same & (s <= t)]
    bwd = [same & (s >= t)]
    for m in _gla_levels():
        parent = blk(t, 2 * m) == blk(s, 2 * m)
        t_odd = (blk(t, m) & 1) == 1
        s_odd = (blk(s, m) & 1) == 1
        fwd.append(parent & t_odd & ~s_odd)
        bwd.append(parent & ~t_odd & s_odd)
    return fwd, bwd


def _score_operands(q, k, b, *, forward):
    sign = 1.0 if forward else -1.0
    half = GLA_BLOCK // 2
    r = _ref_rows(b, half - 1 if forward else half, GLA_BLOCK)
    pairs = [(q * jnp.exp(sign * (b - r)), k * jnp.exp(sign * (r - b)))]
    for m in _gla_levels():
        r = _ref_rows(b, m - 1 if forward else m, 2 * m)
        pairs.append((q * jnp.exp(jnp.minimum(sign * (b - r), 0.0)),
                      k * jnp.exp(jnp.minimum(sign * (r - b), 0.0))))
    return [(a.astype(BF16), c.astype(BF16)) for a, c in pairs]


def _mix1_kernel(x_ref, mod_ref, gn_ref, win_ref, ws_ref, bs_ref, gv_ref, lbf_ref, lbb_ref, s0_ref,
                 a_ref, og_ref, op_ref, qb_ref, kb_ref, v_ref, dg_ref, sf_ref,
                 z_ref, q_s, kf_s, gf_s, kbs_s, gb_s, st_ref, *, tm):
    C = GLA_CHUNK
    t_idx = pl.program_id(1)

    @pl.when(t_idx == 0)
    def _():
        st_ref[...] = s0_ref[...]

    x = x_ref[...]
    shift = mod_ref[3:4, :]
    scale = mod_ref[4:5, :]
    h = (_rms(x, gn_ref[...]) * (1.0 + scale) + shift).astype(BF16)
    z_ref[...] = _dot(h, win_ref[...])

    for cc in range(tm // GMLP_CHUNK):
        rows = slice(cc * GMLP_CHUNK, (cc + 1) * GMLP_CHUNK)
        for g in range(N_GROUPS):
            cu = slice(g * GROUP_W, (g + 1) * GROUP_W)
            cv = slice(A_W + g * GROUP_W, A_W + (g + 1) * GROUP_W)
            vn = _rms(_gelu_tanh(z_ref[rows, cv]), gv_ref[:, cu])
            sv = _dot(ws_ref[g], vn.astype(BF16)) + bs_ref[g]
            a_ref[rows, cu] = (_gelu_tanh(z_ref[rows, cu]) * sv).astype(BF16)

    c0 = 2 * A_W
    q_s[...] = _silu(z_ref[:, c0:c0 + B_W])
    kf, gf = _forget(z_ref[:, c0 + B_W:c0 + 2 * B_W], lbf_ref[...])
    kf_s[...] = kf
    gf_s[...] = gf
    kb, gb = _forget(z_ref[:, c0 + 2 * B_W:c0 + 3 * B_W], lbb_ref[...])
    kbs_s[...] = kb
    gb_s[...] = gb
    v_ref[...] = z_ref[:, c0 + 3 * B_W:c0 + 4 * B_W].astype(BF16)
    og_ref[...] = _silu(z_ref[:, c0 + 4 * B_W:c0 + 5 * B_W]).astype(BF16)

    ti = lax.broadcasted_iota(jnp.int32, (C, C), 0)
    si = lax.broadcasted_iota(jnp.int32, (C, C), 1)
    tri = (si <= ti).astype(BF16)
    masks_f, masks_b = _gla_masks()

    def chunk(c, carry):
        rs = pl.ds(pl.multiple_of(c * C, C), C)
        q = q_s[rs, :]
        kf = kf_s[rs, :]
        kb = kbs_s[rs, :]
        gb = gb_s[rs, :]
        cs = _cumsum_rows(jnp.concatenate([gf_s[rs, :], gb], axis=1), tri)
        bf = cs[:, :B_W]
        cb = cs[:, B_W:]
        eb = cb - gb
        v = v_ref[rs, :]

        ops_f = _score_operands(q, kf, bf, forward=True)
        ops_b = _score_operands(q, kb, eb, forward=False)
        bf_end = bf[C - 1:C, :]
        gb_tot = cb[C - 1:C, :]
        q_in = (q * jnp.exp(bf)).astype(BF16)
        k_out = (kf * jnp.exp(bf_end - bf)).astype(BF16)
        dec_f = jnp.exp(bf_end)
        qb_ref[rs, :] = (q * jnp.exp(gb_tot - eb)).astype(BF16)
        kb_ref[rs, :] = (kb * jnp.exp(eb)).astype(BF16)
        dg_ref[pl.ds(c, 1), :] = jnp.exp(gb_tot)

        for hd in range(N_HEADS):
            hs = slice(hd * GROUP_W, (hd + 1) * GROUP_W)
            scores = jnp.zeros((C, C), F32)
            for (qa, ka), mk in zip(ops_f, masks_f):
                scores = scores + jnp.where(mk, _dot_nt(qa[:, hs], ka[:, hs]), 0.0)
            for (qa, ka), mk in zip(ops_b, masks_b):
                scores = scores + jnp.where(mk, _dot_nt(qa[:, hs], ka[:, hs]), 0.0)
            st = st_ref[hd]
            o = _dot(scores.astype(BF16), v[:, hs]) + _dot_nt(q_in[:, hs], st.astype(BF16))
            op_ref[rs, hs] = o
            st_ref[hd] = dec_f[:, hs] * st + _dot_tn(v[:, hs], k_out[:, hs])
        return carry

    lax.fori_loop(0, tm // C, chunk, 0)

    @pl.when(t_idx == pl.num_programs(1) - 1)
    def _():
        sf_ref[...] = st_ref[...]


def _mix1(x, mod, mod_row, gn, w_in, ws, bs, gv, lbf, lbb, s0, *, tm):
    B, L, _ = x.shape
    n_t = L // tm
    n_c = tm // GLA_CHUNK
    tok = lambda w: pl.BlockSpec((None, tm, w), lambda b, t: (b, t, 0))
    state = pl.BlockSpec((None, N_HEADS, GROUP_W, GROUP_W), lambda b, t: (b, 0, 0, 0))
    half = jax.ShapeDtypeStruct((B, L, B_W), BF16)
    return pl.pallas_call(
        functools.partial(_mix1_kernel, tm=tm),
        grid=(B, n_t),
        in_specs=[
            tok(D_MODEL),
            pl.BlockSpec((None, N_MOD, D_MODEL), lambda b, t: (mod_row(b), 0, 0)),
            _const_spec((1, D_MODEL)),
            _const_spec((D_MODEL, IN_W)),
            _const_spec((N_GROUPS, GMLP_CHUNK, GMLP_CHUNK)),
            _const_spec((N_GROUPS, GMLP_CHUNK, GROUP_W)),
            _const_spec((1, A_W)),
            _const_spec((1, B_W)),
            _const_spec((1, B_W)),
            state,
        ],
        out_specs=[
            tok(A_W), tok(B_W), tok(B_W), tok(B_W), tok(B_W), tok(B_W),
            pl.BlockSpec((None, None, n_c, B_W), lambda b, t: (b, t, 0, 0)),
            state,
        ],
        out_shape=[
            half, half, jax.ShapeDtypeStruct((B, L, B_W), F32), half, half, half,
            jax.ShapeDtypeStruct((B, n_t, n_c, B_W), F32),
            jax.ShapeDtypeStruct((B, N_HEADS, GROUP_W, GROUP_W), F32),
        ],
        scratch_shapes=[
            pltpu.VMEM((tm, IN_W), F32),
            pltpu.VMEM((tm, B_W), F32),
            pltpu.VMEM((tm, B_W), F32),
            pltpu.VMEM((tm, B_W), F32),
            pltpu.VMEM((tm, B_W), F32),
            pltpu.VMEM((tm, B_W), F32),
            pltpu.VMEM((N_HEADS, GROUP_W, GROUP_W), F32),
        ],
        compiler_params=pltpu.CompilerParams(
            dimension_semantics=("arbitrary", "arbitrary"), vmem_limit_bytes=VMEM_LIMIT),
        name="mixer_forward",
    )(x, mod, gn.reshape(1, D_MODEL), w_in, ws, bs, gv.reshape(1, A_W),
      lbf.reshape(1, B_W), lbb.reshape(1, B_W), s0)


def _mix2_kernel(x_ref, mod_ref, a_ref, og_ref, op_ref, qb_ref, kb_ref, v_ref, dg_ref, go_ref,
                 wout_ref, s0_ref, o_ref, sb_ref, o_s, st_ref, *, tm):
    C = GLA_CHUNK
    n_c = tm // C
    t_idx = pl.program_id(1)

    @pl.when(t_idx == 0)
    def _():
        st_ref[...] = s0_ref[...]

    def chunk(i, carry):
        c = n_c - 1 - i
        rs = pl.ds(pl.multiple_of(c * C, C), C)
        qb = qb_ref[rs, :]
        kb = kb_ref[rs, :]
        v = v_ref[rs, :]
        dg = dg_ref[pl.ds(c, 1), :]
        for hd in range(N_HEADS):
            hs = slice(hd * GROUP_W, (hd + 1) * GROUP_W)
            st = st_ref[hd]
            o_s[rs, hs] = op_ref[rs, hs] + _dot_nt(qb[:, hs], st.astype(BF16))
            st_ref[hd] = dg[:, hs] * st + _dot_tn(v[:, hs], kb[:, hs])
        return carry

    lax.fori_loop(0, n_c, chunk, 0)

    @pl.when(t_idx == pl.num_programs(1) - 1)
    def _():
        sb_ref[...] = st_ref[...]

    y = _dot(a_ref[...], wout_ref[0:A_W, :])
    for hd in range(N_HEADS):
        hs = slice(hd * GROUP_W, (hd + 1) * GROUP_W)
        on = _rms(o_s[:, hs], go_ref[:, hs]) * og_ref[:, hs].astype(F32)
        y = y + _dot(on.astype(BF16), wout_ref[A_W + hd * GROUP_W:A_W + (hd + 1) * GROUP_W, :])
    o_ref[...] = x_ref[...] + mod_ref[5:6, :] * y


def _mix2(x, mod, mod_row, a, og, op, qb, kb, v, dg, go, w_out, s0, *, tm):
    B, L, _ = x.shape
    n_t = L // tm
    n_c = tm // GLA_CHUNK
    tok = lambda w: pl.BlockSpec((None, tm, w), lambda b, t: (b, n_t - 1 - t, 0))
    state = pl.BlockSpec((None, N_HEADS, GROUP_W, GROUP_W), lambda b, t: (b, 0, 0, 0))
    return pl.pallas_call(
        functools.partial(_mix2_kernel, tm=tm),
        grid=(B, n_t),
        in_specs=[
            tok(D_MODEL),
            pl.BlockSpec((None, N_MOD, D_MODEL), lambda b, t: (mod_row(b), 0, 0)),
            tok(A_W), tok(B_W), tok(B_W), tok(B_W), tok(B_W), tok(B_W),
            pl.BlockSpec((None, None, n_c, B_W), lambda b, t: (b, n_t - 1 - t, 0, 0)),
            _const_spec((1, B_W)),
            _const_spec((D_MODEL, D_MODEL)),
            state,
        ],
        out_specs=[tok(D_MODEL), state],
        out_shape=[
            jax.ShapeDtypeStruct(x.shape, F32),
            jax.ShapeDtypeStruct((B, N_HEADS, GROUP_W, GROUP_W), F32),
        ],
        scratch_shapes=[
            pltpu.VMEM((tm, B_W), F32),
            pltpu.VMEM((N_HEADS, GROUP_W, GROUP_W), F32),
        ],
        compiler_params=pltpu.CompilerParams(
            dimension_semantics=("arbitrary", "arbitrary"), vmem_limit_bytes=VMEM_LIMIT),
        name="mixer_backward",
    )(x, mod, a, og, op, qb, kb, v, dg, go.reshape(1, B_W), w_out, s0)


def _grid_pos_embed(rows):
    row = jnp.repeat(jnp.arange(rows), GRID_W).astype(F32)
    col = jnp.tile(jnp.arange(GRID_W), rows).astype(F32)
    quarter = D_MODEL // 4
    freq = POS_THETA ** (-jnp.arange(quarter, dtype=F32) / quarter)

    def axis_embed(p):
        a = p[:, None] * freq[None, :]
        return jnp.concatenate([jnp.sin(a), jnp.cos(a)], axis=-1)

    return jnp.concatenate([axis_embed(row), axis_embed(col)], axis=-1)


def _lower_bounds(p):
    cum = jnp.cumsum(jax.nn.softmax(p.astype(F32), axis=0), axis=0)
    return cum - cum[0:1]


def _tile(L, want):
    tm = min(L, want)
    assert L % tm == 0 and tm % GMLP_CHUNK == 0 and tm % GLA_CHUNK == 0
    return tm


def kernel(x, c, ctx, c_ctx, w_ada, b_ada, norm_ffn1_g, norm_mix_g, norm_ffn2_g, ffn1_w1, ffn1_w3,
           ffn1_w2, ffn2_w1, ffn2_w3, ffn2_w2, w_in, w_out, gmlp_ws, gmlp_bs, gmlp_norm_g,
           hgrn_lb_fwd, hgrn_lb_bwd, hgrn_norm_g, norm_final_g):
    B, L, _ = x.shape
    n_ctx = ctx.shape[1]
    assert B + 1 <= MOD_ROWS
    tm_lat = _tile(L, 512)
    tm_ctx = _tile(n_ctx, 512)

    cond = jnp.concatenate([c, c_ctx[None], jnp.zeros((MOD_ROWS - B - 1, D_MODEL), F32)], axis=0)
    mods = _modulation(cond, w_ada, b_ada).reshape(DEPTH, MOD_ROWS, N_MOD, D_MODEL)
    lat_row = lambda b: b
    ctx_row = lambda b: B

    pos = _grid_pos_embed(L // GRID_W)
    lbs_f = _lower_bounds(hgrn_lb_fwd)
    lbs_b = _lower_bounds(hgrn_lb_bwd)
    bs_full = jnp.broadcast_to(gmlp_bs[..., None], gmlp_bs.shape + (GROUP_W,))
    ffn1 = [w.astype(BF16) for w in (ffn1_w1, ffn1_w3, ffn1_w2)]
    ffn2 = [w.astype(BF16) for w in (ffn2_w1, ffn2_w3, ffn2_w2)]
    w_in_h = w_in.astype(BF16)
    w_out_h = w_out.astype(BF16)
    ws_h = gmlp_ws.astype(BF16)
    zero_state = jnp.zeros((B, N_HEADS, GROUP_W, GROUP_W), F32)

    lat, cx = x, ctx
    for l in range(DEPTH):
        last = l == DEPTH - 1
        mod = mods[l]
        w1, w3, w2 = (w[l] for w in ffn1)
        lat = _ffn(lat, mod, lat_row, norm_ffn1_g[l], w1, w3, w2, j0=0, tm=tm_lat,
                   pos=pos if l == 0 else None)
        cx = _ffn(cx, mod, ctx_row, norm_ffn1_g[l], w1, w3, w2, j0=0, tm=tm_ctx)

        mix1 = functools.partial(_mix1, gn=norm_mix_g[l], w_in=w_in_h[l], ws=ws_h[l], bs=bs_full[l],
                                 gv=gmlp_norm_g[l], lbf=lbs_f[l], lbb=lbs_b[l])
        mix2 = functools.partial(_mix2, go=hgrn_norm_g[l], w_out=w_out_h[l])
        *parts_c, s_f = mix1(cx, mod, ctx_row, s0=zero_state, tm=tm_ctx)
        cx_mixed, s_b = mix2(cx, mod, ctx_row, *parts_c, s0=zero_state, tm=tm_ctx)
        *parts_l, _ = mix1(lat, mod, lat_row, s0=s_f, tm=tm_lat)
        lat, _ = mix2(lat, mod, lat_row, *parts_l, s0=s_b, tm=tm_lat)

        w1, w3, w2 = (w[l] for w in ffn2)
        lat = _ffn(lat, mod, lat_row, norm_ffn2_g[l], w1, w3, w2, j0=6, tm=tm_lat,
                   gfin=norm_final_g if last else None)
        if not last:
            cx = _ffn(cx_mixed, mod, ctx_row, norm_ffn2_g[l], w1, w3, w2, j0=6, tm=tm_ctx)
    return lat
```

```python
import functools

import jax
import jax.numpy as jnp
from jax import lax
from jax.experimental import pallas as pl
from jax.experimental.pallas import tpu as pltpu

D_MODEL = 1024
D_FF = 2816
DEPTH = 4
GRID_W = 64
A_W = 512
B_W = 512
IN_W = 2 * A_W + 5 * B_W
N_GROUPS = 4
GROUP_W = 128
GMLP_CHUNK = 128
N_HEADS = 4
N_MOD = 9
MOD_ROWS = 16
EPS = 1e-6
POS_THETA = 10000.0

GLA_CHUNK = 64
GLA_BLOCK = 16
LOG_F_MIN = -10.0
FF_SPLITS = ((0, 1536), (1536, 1280))

V7X_VMEM_BYTES = 64 * 1024 * 1024
VMEM_LIMIT = V7X_VMEM_BYTES - 8 * 1024 * 1024

F32 = jnp.float32
BF16 = jnp.bfloat16


def _silu(x):
    hx = 0.5 * x
    return hx + hx * jnp.tanh(hx)


def _gelu_tanh(x):
    hx = 0.5 * x
    return hx + hx * jnp.tanh(x * (0.7978845608028654 + 0.035677408136300125 * (x * x)))


def _rms(x, g):
    return x * lax.rsqrt(jnp.mean(x * x, axis=-1, keepdims=True) + EPS) * g


def _dot(a, b):
    return jnp.dot(a, b, preferred_element_type=F32)


def _dot_nt(a, b):
    return lax.dot_general(a, b, (((1,), (1,)), ((), ())), preferred_element_type=F32)


def _dot_tn(a, b):
    return lax.dot_general(a, b, (((0,), (0,)), ((), ())), preferred_element_type=F32)


def _layer_spec(layer, shape):
    nd = len(shape)
    return pl.BlockSpec((None,) + shape, lambda *_: (layer,) + (0,) * nd, pipeline_mode=pl.Buffered(1))


def _mod_spec(layer, mod_row):
    return pl.BlockSpec((None, None, N_MOD, D_MODEL), lambda b, t: (layer, mod_row(b), 0, 0))


def _mod_kernel(cond_ref, w_ref, b_ref, o_ref):
    s = _silu(cond_ref[...])
    o_ref[...] = jnp.dot(s, w_ref[...], preferred_element_type=F32,
                         precision=lax.Precision.HIGHEST) + b_ref[...]


def _modulation(cond, w_ada, b_ada):
    n_out = N_MOD * D_MODEL
    return pl.pallas_call(
        _mod_kernel,
        grid=(DEPTH, N_MOD),
        in_specs=[
            pl.BlockSpec((MOD_ROWS, D_MODEL), lambda l, j: (0, 0)),
            pl.BlockSpec((None, D_MODEL, D_MODEL), lambda l, j: (l, 0, j)),
            pl.BlockSpec((None, 1, D_MODEL), lambda l, j: (l, 0, j)),
        ],
        out_specs=pl.BlockSpec((None, MOD_ROWS, D_MODEL), lambda l, j: (l, 0, j)),
        out_shape=jax.ShapeDtypeStruct((DEPTH, MOD_ROWS, n_out), F32),
        compiler_params=pltpu.CompilerParams(dimension_semantics=("arbitrary", "arbitrary")),
        name="adaln_modulation",
    )(cond, w_ada, b_ada.reshape(DEPTH, 1, n_out))


def _ffn_kernel(*refs, j0, add_pos, final):
    refs = list(refs)
    x_ref = refs.pop(0)
    pos_ref = refs.pop(0) if add_pos else None
    mod_ref, g_ref, w1_ref, w3_ref, w2_ref = refs[:5]
    refs = refs[5:]
    gfin_ref = refs.pop(0) if final else None
    o_ref = refs.pop(0)

    x = x_ref[...]
    if add_pos:
        x = x + pos_ref[...]
    shift = mod_ref[j0:j0 + 1, :]
    scale = mod_ref[j0 + 1:j0 + 2, :]
    gate = mod_ref[j0 + 2:j0 + 3, :]
    h = (_rms(x, g_ref[...]) * (1.0 + scale) + shift).astype(BF16)
    y = None
    for s, n in FF_SPLITS:
        a = _dot(h, w1_ref[:, s:s + n])
        b = _dot(h, w3_ref[:, s:s + n])
        p = (_silu(a) * b).astype(BF16)
        yc = _dot(p, w2_ref[s:s + n, :])
        y = yc if y is None else y + yc
    out = x + (0.5 * gate) * y
    if final:
        out = _rms(out, gfin_ref[...])
    o_ref[...] = out


def _ffn(x, mods, mod_row, g, w1, w3, w2, *, layer, j0, tm, pos=None, gfin=None):
    B, L, _ = x.shape
    add_pos, final = pos is not None, gfin is not None
    tok = pl.BlockSpec((None, tm, D_MODEL), lambda b, t: (b, t, 0))
    in_specs, args = [tok], [x]
    if add_pos:
        in_specs.append(pl.BlockSpec((tm, D_MODEL), lambda b, t: (t, 0)))
        args.append(pos)
    in_specs += [
        _mod_spec(layer, mod_row),
        _layer_spec(layer, (1, D_MODEL)),
        _layer_spec(layer, (D_MODEL, D_FF)),
        _layer_spec(layer, (D_MODEL, D_FF)),
        _layer_spec(layer, (D_FF, D_MODEL)),
    ]
    args += [mods, g.reshape(DEPTH, 1, D_MODEL), w1, w3, w2]
    if final:
        in_specs.append(pl.BlockSpec((1, D_MODEL), lambda b, t: (0, 0)))
        args.append(gfin.reshape(1, D_MODEL))
    return pl.pallas_call(
        functools.partial(_ffn_kernel, j0=j0, add_pos=add_pos, final=final),
        grid=(B, L // tm),
        in_specs=in_specs,
        out_specs=tok,
        out_shape=jax.ShapeDtypeStruct(x.shape, F32),
        compiler_params=pltpu.CompilerParams(
            dimension_semantics=("parallel", "parallel"), vmem_limit_bytes=VMEM_LIMIT),
        name="adaln_swiglu",
    )(*args)


def _forget(f_raw, lb):
    t = jnp.tanh(0.5 * f_raw)
    half = 0.5 * (1.0 - lb)
    k = half - half * t
    log_f = jnp.maximum(jnp.log((lb + half) + half * t), LOG_F_MIN)
    return k, log_f


def _cumsum_rows(g, tri):
    hi = g.astype(BF16)
    mid = (g - hi.astype(F32)).astype(BF16)
    return _dot(tri, hi) + _dot(tri, mid)


def _ref_rows(a, first, period):
    n = a.shape[0] // period
    parts = [jnp.broadcast_to(a[first + j * period:first + j * period + 1, :], (period, a.shape[1]))
             for j in range(n)]
    return parts[0] if n == 1 else jnp.concatenate(parts, axis=0)


def _gla_levels():
    levels = []
    m = GLA_BLOCK
    while m < GLA_CHUNK:
        levels.append(m)
        m *= 2
    return tuple(levels)


def _interleave_rows(first, second, m):
    C = first.shape[0]
    parts = [(first if (i // m) % 2 == 0 else second)[i:i + m] for i in range(0, C, m)]
    return jnp.concatenate(parts, axis=0)


def _keep_rows(a, lo, hi):
    parts = []
    if lo > 0:
        parts.append(jnp.zeros((lo, a.shape[1]), a.dtype))
    parts.append(a[lo:hi])
    if hi < a.shape[0]:
        parts.append(jnp.zeros((a.shape[0] - hi, a.shape[1]), a.dtype))
    return parts[0] if len(parts) == 1 else jnp.concatenate(parts, axis=0)


def _score_operands(q, k, b, *, forward):
    half = GLA_BLOCK // 2
    d = b - _ref_rows(b, half - 1 if forward else half, GLA_BLOCK)
    if not forward:
        d = -d
    diag = ((q * jnp.exp(d)).astype(BF16), (k * jnp.exp(-d)).astype(BF16))
    levels = []
    for m in _gla_levels():
        r = _ref_rows(b, m - 1 if forward else m, 2 * m)
        qk = _interleave_rows(k, q, m) if forward else _interleave_rows(q, k, m)
        levels.append((qk * jnp.exp(-jnp.abs(b - r))).astype(BF16))
    return diag, levels


def _level_columns(lev_f, lev_b, hs):
    C = GLA_CHUNK
    q_cols, k_cols = [], []
    for m, lf, lb in zip(_gla_levels(), lev_f, lev_b):
        for lo in range(0, C, 2 * m):
            mid, hi = lo + m, lo + 2 * m
            q_cols.append(_keep_rows(lf[:, hs], mid, hi))
            k_cols.append(_keep_rows(lf[:, hs], lo, mid))
            q_cols.append(_keep_rows(lb[:, hs], lo, mid))
            k_cols.append(_keep_rows(lb[:, hs], mid, hi))
    return jnp.concatenate(q_cols, axis=1), jnp.concatenate(k_cols, axis=1)


def _mix1_kernel(x_ref, mod_ref, gn_ref, win_ref, ws_ref, bs_ref, gv_ref, lbf_ref, lbb_ref, s0_ref,
                 a_ref, og_ref, op_ref, qb_ref, kb_ref, v_ref, dg_ref, sf_ref,
                 z_ref, q_s, kf_s, gf_s, kbs_s, gb_s, st_ref, *, tm):
    C = GLA_CHUNK
    t_idx = pl.program_id(1)

    @pl.when(t_idx == 0)
    def _():
        st_ref[...] = s0_ref[...]

    x = x_ref[...]
    shift = mod_ref[3:4, :]
    scale = mod_ref[4:5, :]
    h = (_rms(x, gn_ref[...]) * (1.0 + scale) + shift).astype(BF16)
    z_ref[...] = _dot(h, win_ref[...])

    for cc in range(tm // GMLP_CHUNK):
        rows = slice(cc * GMLP_CHUNK, (cc + 1) * GMLP_CHUNK)
        for g in range(N_GROUPS):
            cu = slice(g * GROUP_W, (g + 1) * GROUP_W)
            cv = slice(A_W + g * GROUP_W, A_W + (g + 1) * GROUP_W)
            vn = _rms(_gelu_tanh(z_ref[rows, cv]), gv_ref[:, cu])
            sv = _dot(ws_ref[g], vn.astype(BF16)) + bs_ref[g]
            a_ref[rows, cu] = (_gelu_tanh(z_ref[rows, cu]) * sv).astype(BF16)

    c0 = 2 * A_W
    q_s[...] = _silu(z_ref[:, c0:c0 + B_W])
    kf, gf = _forget(z_ref[:, c0 + B_W:c0 + 2 * B_W], lbf_ref[...])
    kf_s[...] = kf
    gf_s[...] = gf
    kb, gb = _forget(z_ref[:, c0 + 2 * B_W:c0 + 3 * B_W], lbb_ref[...])
    kbs_s[...] = kb
    gb_s[...] = gb
    v_ref[...] = z_ref[:, c0 + 3 * B_W:c0 + 4 * B_W].astype(BF16)
    og_ref[...] = _silu(z_ref[:, c0 + 4 * B_W:c0 + 5 * B_W]).astype(BF16)

    ti = lax.broadcasted_iota(jnp.int32, (C, C), 0)
    si = lax.broadcasted_iota(jnp.int32, (C, C), 1)
    tri = (si <= ti).astype(BF16)
    shift_blk = GLA_BLOCK.bit_length() - 1
    same_blk = lax.shift_right_logical(ti, shift_blk) == lax.shift_right_logical(si, shift_blk)
    diag_f = same_blk & (si <= ti)
    diag_b = same_blk & (si >= ti)

    for c in range(tm // C):
        rs = slice(c * C, (c + 1) * C)
        q = q_s[rs, :]
        kf = kf_s[rs, :]
        kb = kbs_s[rs, :]
        gb = gb_s[rs, :]
        cs = _cumsum_rows(jnp.concatenate([gf_s[rs, :], gb], axis=1), tri)
        bf = cs[:, :B_W]
        cb = cs[:, B_W:]
        eb = cb - gb
        v = v_ref[rs, :]

        (qd_f, kd_f), lev_f = _score_operands(q, kf, bf, forward=True)
        (qd_b, kd_b), lev_b = _score_operands(q, kb, eb, forward=False)
        bf_end = bf[C - 1:C, :]
        gb_tot = cb[C - 1:C, :]
        q_in = (q * jnp.exp(bf)).astype(BF16)
        k_out = (kf * jnp.exp(bf_end - bf)).astype(BF16)
        dec_f = jnp.exp(bf_end)
        qb_ref[rs, :] = (q * jnp.exp(gb_tot - eb)).astype(BF16)
        kb_ref[rs, :] = (kb * jnp.exp(eb)).astype(BF16)
        dg_ref[c:c + 1, :] = jnp.exp(gb_tot)

        for hd in range(N_HEADS):
            hs = slice(hd * GROUP_W, (hd + 1) * GROUP_W)
            q_lv, k_lv = _level_columns(lev_f, lev_b, hs)
            scores = (_dot_nt(q_lv, k_lv)
                      + jnp.where(diag_f, _dot_nt(qd_f[:, hs], kd_f[:, hs]), 0.0)
                      + jnp.where(diag_b, _dot_nt(qd_b[:, hs], kd_b[:, hs]), 0.0))
            st = st_ref[hd]
            o = _dot(scores.astype(BF16), v[:, hs]) + _dot_nt(q_in[:, hs], st.astype(BF16))
            op_ref[rs, hs] = o
            st_ref[hd] = dec_f[:, hs] * st + _dot_tn(v[:, hs], k_out[:, hs])

    @pl.when(t_idx == pl.num_programs(1) - 1)
    def _():
        sf_ref[...] = st_ref[...]


def _mix1(x, mods, mod_row, gn, w_in, ws, bs, gv, lbf, lbb, s0, *, layer, tm):
    B, L, _ = x.shape
    n_t = L // tm
    n_c = tm // GLA_CHUNK
    tok = lambda w: pl.BlockSpec((None, tm, w), lambda b, t: (b, t, 0))
    state = pl.BlockSpec((None, N_HEADS, GROUP_W, GROUP_W), lambda b, t: (b, 0, 0, 0))
    half = jax.ShapeDtypeStruct((B, L, B_W), BF16)
    return pl.pallas_call(
        functools.partial(_mix1_kernel, tm=tm),
        grid=(B, n_t),
        in_specs=[
            tok(D_MODEL),
            _mod_spec(layer, mod_row),
            _layer_spec(layer, (1, D_MODEL)),
            _layer_spec(layer, (D_MODEL, IN_W)),
            _layer_spec(layer, (N_GROUPS, GMLP_CHUNK, GMLP_CHUNK)),
            _layer_spec(layer, (N_GROUPS, GMLP_CHUNK, GROUP_W)),
            _layer_spec(layer, (1, A_W)),
            _layer_spec(layer, (1, B_W)),
            _layer_spec(layer, (1, B_W)),
            state,
        ],
        out_specs=[
            tok(A_W), tok(B_W), tok(B_W), tok(B_W), tok(B_W), tok(B_W),
            pl.BlockSpec((None, None, n_c, B_W), lambda b, t: (b, t, 0, 0)),
            state,
        ],
        out_shape=[
            half, half, jax.ShapeDtypeStruct((B, L, B_W), F32), half, half, half,
            jax.ShapeDtypeStruct((B, n_t, n_c, B_W), F32),
            jax.ShapeDtypeStruct((B, N_HEADS, GROUP_W, GROUP_W), F32),
        ],
        scratch_shapes=[
            pltpu.VMEM((tm, IN_W), F32),
            pltpu.VMEM((tm, B_W), F32),
            pltpu.VMEM((tm, B_W), F32),
            pltpu.VMEM((tm, B_W), F32),
            pltpu.VMEM((tm, B_W), F32),
            pltpu.VMEM((tm, B_W), F32),
            pltpu.VMEM((N_HEADS, GROUP_W, GROUP_W), F32),
        ],
        compiler_params=pltpu.CompilerParams(
            dimension_semantics=("arbitrary", "arbitrary"), vmem_limit_bytes=VMEM_LIMIT),
        name="mixer_forward",
    )(x, mods, gn.reshape(DEPTH, 1, D_MODEL), w_in, ws, bs, gv.reshape(DEPTH, 1, A_W),
      lbf.reshape(DEPTH, 1, B_W), lbb.reshape(DEPTH, 1, B_W), s0)


def _mix2_kernel(x_ref, mod_ref, a_ref, og_ref, op_ref, qb_ref, kb_ref, v_ref, dg_ref, go_ref,
                 wout_ref, s0_ref, o_ref, sb_ref, o_s, st_ref, *, tm):
    C = GLA_CHUNK
    n_c = tm // C
    t_idx = pl.program_id(1)

    @pl.when(t_idx == 0)
    def _():
        st_ref[...] = s0_ref[...]

    for c in reversed(range(n_c)):
        rs = slice(c * C, (c + 1) * C)
        qb = qb_ref[rs, :]
        kb = kb_ref[rs, :]
        v = v_ref[rs, :]
        dg = dg_ref[c:c + 1, :]
        for hd in range(N_HEADS):
            hs = slice(hd * GROUP_W, (hd + 1) * GROUP_W)
            st = st_ref[hd]
            o_s[rs, hs] = op_ref[rs, hs] + _dot_nt(qb[:, hs], st.astype(BF16))
            st_ref[hd] = dg[:, hs] * st + _dot_tn(v[:, hs], kb[:, hs])

    @pl.when(t_idx == pl.num_programs(1) - 1)
    def _():
        sb_ref[...] = st_ref[...]

    y = _dot(a_ref[...], wout_ref[0:A_W, :])
    for hd in range(N_HEADS):
        hs = slice(hd * GROUP_W, (hd + 1) * GROUP_W)
        on = _rms(o_s[:, hs], go_ref[:, hs]) * og_ref[:, hs].astype(F32)
        y = y + _dot(on.astype(BF16), wout_ref[A_W + hd * GROUP_W:A_W + (hd + 1) * GROUP_W, :])
    o_ref[...] = x_ref[...] + mod_ref[5:6, :] * y


def _mix2(x, mods, mod_row, a, og, op, qb, kb, v, dg, go, w_out, s0, *, layer, tm):
    B, L, _ = x.shape
    n_t = L // tm
    n_c = tm // GLA_CHUNK
    tok = lambda w: pl.BlockSpec((None, tm, w), lambda b, t: (b, n_t - 1 - t, 0))
    state = pl.BlockSpec((None, N_HEADS, GROUP_W, GROUP_W), lambda b, t: (b, 0, 0, 0))
    return pl.pallas_call(
        functools.partial(_mix2_kernel, tm=tm),
        grid=(B, n_t),
        in_specs=[
            tok(D_MODEL),
            _mod_spec(layer, mod_row),
            tok(A_W), tok(B_W), tok(B_W), tok(B_W), tok(B_W), tok(B_W),
            pl.BlockSpec((None, None, n_c, B_W), lambda b, t: (b, n_t - 1 - t, 0, 0)),
            _layer_spec(layer, (1, B_W)),
            _layer_spec(layer, (D_MODEL, D_MODEL)),
            state,
        ],
        out_specs=[tok(D_MODEL), state],
        out_shape=[
            jax.ShapeDtypeStruct(x.shape, F32),
            jax.ShapeDtypeStruct((B, N_HEADS, GROUP_W, GROUP_W), F32),
        ],
        scratch_shapes=[
            pltpu.VMEM((tm, B_W), F32),
            pltpu.VMEM((N_HEADS, GROUP_W, GROUP_W), F32),
        ],
        compiler_params=pltpu.CompilerParams(
            dimension_semantics=("arbitrary", "arbitrary"), vmem_limit_bytes=VMEM_LIMIT),
        name="mixer_backward",
    )(x, mods, a, og, op, qb, kb, v, dg, go.reshape(DEPTH, 1, B_W), w_out, s0)


def _grid_pos_embed(rows):
    row = jnp.repeat(jnp.arange(rows), GRID_W).astype(F32)
    col = jnp.tile(jnp.arange(GRID_W), rows).astype(F32)
    quarter = D_MODEL // 4
    freq = POS_THETA ** (-jnp.arange(quarter, dtype=F32) / quarter)

    def axis_embed(p):
        a = p[:, None] * freq[None, :]
        return jnp.concatenate([jnp.sin(a), jnp.cos(a)], axis=-1)

    return jnp.concatenate([axis_embed(row), axis_embed(col)], axis=-1)


def _lower_bounds(p):
    cum = jnp.cumsum(jax.nn.softmax(p.astype(F32), axis=0), axis=0)
    return cum - cum[0:1]


def _tile(L, want):
    tm = min(L, want)
    assert L % tm == 0 and tm % GMLP_CHUNK == 0 and tm % GLA_CHUNK == 0
    return tm


def kernel(x, c, ctx, c_ctx, w_ada, b_ada, norm_ffn1_g, norm_mix_g, norm_ffn2_g, ffn1_w1, ffn1_w3,
           ffn1_w2, ffn2_w1, ffn2_w3, ffn2_w2, w_in, w_out, gmlp_ws, gmlp_bs, gmlp_norm_g,
           hgrn_lb_fwd, hgrn_lb_bwd, hgrn_norm_g, norm_final_g):
    B, L, _ = x.shape
    n_ctx = ctx.shape[1]
    assert B + 1 <= MOD_ROWS
    tm_lat = _tile(L, 512)
    tm_ctx = _tile(n_ctx, 512)

    cond = jnp.concatenate([c, c_ctx[None], jnp.zeros((MOD_ROWS - B - 1, D_MODEL), F32)], axis=0)
    mods = _modulation(cond, w_ada, b_ada).reshape(DEPTH, MOD_ROWS, N_MOD, D_MODEL)
    lat_row = lambda b: b
    ctx_row = lambda b: B

    pos = _grid_pos_embed(L // GRID_W)
    lbs_f = _lower_bounds(hgrn_lb_fwd)
    lbs_b = _lower_bounds(hgrn_lb_bwd)
    bs_full = jnp.broadcast_to(gmlp_bs[..., None], gmlp_bs.shape + (GROUP_W,))
    ffn1 = [w.astype(BF16) for w in (ffn1_w1, ffn1_w3, ffn1_w2)]
    ffn2 = [w.astype(BF16) for w in (ffn2_w1, ffn2_w3, ffn2_w2)]
    zero_state = jnp.zeros((B, N_HEADS, GROUP_W, GROUP_W), F32)
    mix1 = functools.partial(_mix1, gn=norm_mix_g, w_in=w_in.astype(BF16), ws=gmlp_ws.astype(BF16),
                             bs=bs_full, gv=gmlp_norm_g, lbf=lbs_f, lbb=lbs_b)
    mix2 = functools.partial(_mix2, go=hgrn_norm_g, w_out=w_out.astype(BF16))

    lat, cx = x, ctx
    for l in range(DEPTH):
        last = l == DEPTH - 1
        lat = _ffn(lat, mods, lat_row, norm_ffn1_g, *ffn1, layer=l, j0=0, tm=tm_lat,
                   pos=pos if l == 0 else None)
        cx = _ffn(cx, mods, ctx_row, norm_ffn1_g, *ffn1, layer=l, j0=0, tm=tm_ctx)

        *parts_c, s_f = mix1(cx, mods, ctx_row, s0=zero_state, layer=l, tm=tm_ctx)
        cx_mixed, s_b = mix2(cx, mods, ctx_row, *parts_c, s0=zero_state, layer=l, tm=tm_ctx)
        *parts_l, _ = mix1(lat, mods, lat_row, s0=s_f, layer=l, tm=tm_lat)
        lat, _ = mix2(lat, mods, lat_row, *parts_l, s0=s_b, layer=l, tm=tm_lat)

        lat = _ffn(lat, mods, lat_row, norm_ffn2_g, *ffn2, layer=l, j0=6, tm=tm_lat,
                   gfin=norm_final_g if last else None)
        if not last:
            cx = _ffn(cx_mixed, mods, ctx_row, norm_ffn2_g, *ffn2, layer=l, j0=6, tm=tm_ctx)
    return lat
```

```python
import functools

import jax
import jax.numpy as jnp
from jax import lax
from jax.experimental import pallas as pl
from jax.experimental.pallas import tpu as pltpu

D_MODEL = 1024
D_FF = 2816
DEPTH = 4
GRID_W = 64
A_W = 512
B_W = 512
IN_W = 2 * A_W + 5 * B_W
N_GROUPS = 4
GROUP_W = 128
GMLP_CHUNK = 128
N_HEADS = 4
N_MOD = 9
MOD_ROWS = 16
EPS = 1e-6
POS_THETA = 10000.0

GLA_CHUNK = 64
GLA_BLOCK = 16
LOG_F_MIN = -10.0
PREP_LEAD = 2
FF_SPLITS = ((0, 768), (768, 768), (1536, 768), (2304, 512))

V7X_VMEM_BYTES = 64 * 1024 * 1024
VMEM_LIMIT = V7X_VMEM_BYTES - 8 * 1024 * 1024

F32 = jnp.float32
BF16 = jnp.bfloat16


def _silu(x):
    hx = 0.5 * x
    return hx + hx * jnp.tanh(hx)


def _gelu_tanh(x):
    hx = 0.5 * x
    return hx + hx * jnp.tanh(x * (0.7978845608028654 + 0.035677408136300125 * (x * x)))


def _rms(x, g):
    return x * lax.rsqrt(jnp.mean(x * x, axis=-1, keepdims=True) + EPS) * g


def _dot(a, b):
    return jnp.dot(a, b, preferred_element_type=F32)


def _dot_nt(a, b):
    return lax.dot_general(a, b, (((1,), (1,)), ((), ())), preferred_element_type=F32)


def _dot_tn(a, b):
    return lax.dot_general(a, b, (((0,), (0,)), ((), ())), preferred_element_type=F32)


def _layer_spec(layer, shape):
    nd = len(shape)
    return pl.BlockSpec((None,) + shape, lambda *_: (layer,) + (0,) * nd, pipeline_mode=pl.Buffered(1))


def _mod_spec(layer, mod_row):
    return pl.BlockSpec((None, None, N_MOD, D_MODEL), lambda b, t: (layer, mod_row(b), 0, 0))


def _mod_kernel(cond_ref, w_ref, b_ref, o_ref):
    s = _silu(cond_ref[...])
    o_ref[...] = jnp.dot(s, w_ref[...], preferred_element_type=F32,
                         precision=lax.Precision.HIGHEST) + b_ref[...]


def _modulation(cond, w_ada, b_ada):
    n_out = N_MOD * D_MODEL
    return pl.pallas_call(
        _mod_kernel,
        grid=(DEPTH, N_MOD),
        in_specs=[
            pl.BlockSpec((MOD_ROWS, D_MODEL), lambda l, j: (0, 0)),
            pl.BlockSpec((None, D_MODEL, D_MODEL), lambda l, j: (l, 0, j)),
            pl.BlockSpec((None, 1, D_MODEL), lambda l, j: (l, 0, j)),
        ],
        out_specs=pl.BlockSpec((None, MOD_ROWS, D_MODEL), lambda l, j: (l, 0, j)),
        out_shape=jax.ShapeDtypeStruct((DEPTH, MOD_ROWS, n_out), F32),
        compiler_params=pltpu.CompilerParams(dimension_semantics=("arbitrary", "arbitrary")),
        name="adaln_modulation",
    )(cond, w_ada, b_ada.reshape(DEPTH, 1, n_out))


def _ffn_kernel(*refs, j0, add_pos, final):
    refs = list(refs)
    x_ref = refs.pop(0)
    pos_ref = refs.pop(0) if add_pos else None
    mod_ref, g_ref, w1_ref, w3_ref, w2_ref = refs[:5]
    refs = refs[5:]
    gfin_ref = refs.pop(0) if final else None
    o_ref = refs.pop(0)

    x = x_ref[...]
    if add_pos:
        x = x + pos_ref[...]
    shift = mod_ref[j0:j0 + 1, :]
    scale = mod_ref[j0 + 1:j0 + 2, :]
    gate = mod_ref[j0 + 2:j0 + 3, :]
    h = (_rms(x, g_ref[...]) * (1.0 + scale) + shift).astype(BF16)
    y = None
    for s, n in FF_SPLITS:
        a = _dot(h, w1_ref[:, s:s + n])
        b = _dot(h, w3_ref[:, s:s + n])
        p = (_silu(a) * b).astype(BF16)
        yc = _dot(p, w2_ref[s:s + n, :])
        y = yc if y is None else y + yc
    out = x + (0.5 * gate) * y
    if final:
        out = _rms(out, gfin_ref[...])
    o_ref[...] = out


def _ffn(x, mods, mod_row, g, w1, w3, w2, *, layer, j0, tm, pos=None, gfin=None):
    B, L, _ = x.shape
    add_pos, final = pos is not None, gfin is not None
    tok = pl.BlockSpec((None, tm, D_MODEL), lambda b, t: (b, t, 0))
    in_specs, args = [tok], [x]
    if add_pos:
        in_specs.append(pl.BlockSpec((tm, D_MODEL), lambda b, t: (t, 0)))
        args.append(pos)
    in_specs += [
        _mod_spec(layer, mod_row),
        _layer_spec(layer, (1, D_MODEL)),
        _layer_spec(layer, (D_MODEL, D_FF)),
        _layer_spec(layer, (D_MODEL, D_FF)),
        _layer_spec(layer, (D_FF, D_MODEL)),
    ]
    args += [mods, g.reshape(DEPTH, 1, D_MODEL), w1, w3, w2]
    if final:
        in_specs.append(pl.BlockSpec((1, D_MODEL), lambda b, t: (0, 0)))
        args.append(gfin.reshape(1, D_MODEL))
    return pl.pallas_call(
        functools.partial(_ffn_kernel, j0=j0, add_pos=add_pos, final=final),
        grid=(B, L // tm),
        in_specs=in_specs,
        out_specs=tok,
        out_shape=jax.ShapeDtypeStruct(x.shape, F32),
        compiler_params=pltpu.CompilerParams(
            dimension_semantics=("parallel", "parallel"), vmem_limit_bytes=VMEM_LIMIT),
        name="adaln_swiglu",
    )(*args)


def _forget(f_raw, lb):
    t = jnp.tanh(0.5 * f_raw)
    half = 0.5 * (1.0 - lb)
    k = half - half * t
    log_f = jnp.maximum(jnp.log((lb + half) + half * t), LOG_F_MIN)
    return k, log_f


def _cumsum_rows(g, tri):
    hi = g.astype(BF16)
    mid = (g - hi.astype(F32)).astype(BF16)
    return _dot(tri, hi) + _dot(tri, mid)


def _ref_rows(a, first, period):
    n = a.shape[0] // period
    parts = [jnp.broadcast_to(a[first + j * period:first + j * period + 1, :], (period, a.shape[1]))
             for j in range(n)]
    return parts[0] if n == 1 else jnp.concatenate(parts, axis=0)


def _gla_levels():
    levels = []
    m = GLA_BLOCK
    while m < GLA_CHUNK:
        levels.append(m)
        m *= 2
    return tuple(levels)


def _interleave_rows(first, second, m):
    C = first.shape[0]
    parts = [(first if (i // m) % 2 == 0 else second)[i:i + m] for i in range(0, C, m)]
    return jnp.concatenate(parts, axis=0)


def _score_operands(q, k, b, *, forward):
    half = GLA_BLOCK // 2
    d = b - _ref_rows(b, half - 1 if forward else half, GLA_BLOCK)
    if not forward:
        d = -d
    diag = ((q * jnp.exp(d)).astype(BF16), (k * jnp.exp(-d)).astype(BF16))
    levels = []
    for m in _gla_levels():
        r = _ref_rows(b, m - 1 if forward else m, 2 * m)
        qk = _interleave_rows(k, q, m) if forward else _interleave_rows(q, k, m)
        levels.append((qk * jnp.exp(-jnp.abs(b - r))).astype(BF16))
    return diag, levels


def _level_operands(lf, lb, m):
    zf, zb = jnp.zeros_like(lf), jnp.zeros_like(lb)
    q_side = jnp.concatenate([_interleave_rows(zf, lf, m), _interleave_rows(lb, zb, m)], axis=1)
    k_side = jnp.concatenate([_interleave_rows(lf, zf, m), _interleave_rows(zb, lb, m)], axis=1)
    return q_side, k_side


def _mix1_kernel(x_ref, mod_ref, gn_ref, win_ref, ws_ref, bs_ref, gv_ref, lbf_ref, lbb_ref, s0_ref,
                 a_ref, og_ref, op_ref, qb_ref, kb_ref, v_ref, dg_ref, sf_ref, st_ref, *, tm):
    C = GLA_CHUNK
    n_c = tm // C
    t_idx = pl.program_id(1)

    @pl.when(t_idx == 0)
    def _():
        st_ref[...] = s0_ref[...]

    state = [st_ref[hd] for hd in range(N_HEADS)]

    x = x_ref[...]
    shift = mod_ref[3:4, :]
    scale = mod_ref[4:5, :]
    h = (_rms(x, gn_ref[...]) * (1.0 + scale) + shift).astype(BF16)
    z = _dot(h, win_ref[...])

    for cc in range(tm // GMLP_CHUNK):
        rows = slice(cc * GMLP_CHUNK, (cc + 1) * GMLP_CHUNK)
        for g in range(N_GROUPS):
            cu = slice(g * GROUP_W, (g + 1) * GROUP_W)
            cv = slice(A_W + g * GROUP_W, A_W + (g + 1) * GROUP_W)
            vn = _rms(_gelu_tanh(z[rows, cv]), gv_ref[:, cu])
            sv = _dot(ws_ref[g], vn.astype(BF16)) + bs_ref[g]
            a_ref[rows, cu] = (_gelu_tanh(z[rows, cu]) * sv).astype(BF16)

    c0 = 2 * A_W
    q_all = _silu(z[:, c0:c0 + B_W])
    kf_all, gf_all = _forget(z[:, c0 + B_W:c0 + 2 * B_W], lbf_ref[...])
    kb_all, gb_all = _forget(z[:, c0 + 2 * B_W:c0 + 3 * B_W], lbb_ref[...])
    v_all = z[:, c0 + 3 * B_W:c0 + 4 * B_W].astype(BF16)
    v_ref[...] = v_all
    og_ref[...] = _silu(z[:, c0 + 4 * B_W:c0 + 5 * B_W]).astype(BF16)

    ti = lax.broadcasted_iota(jnp.int32, (C, C), 0)
    si = lax.broadcasted_iota(jnp.int32, (C, C), 1)
    tri = (si <= ti).astype(BF16)
    shift_blk = GLA_BLOCK.bit_length() - 1
    same_blk = lax.shift_right_logical(ti, shift_blk) == lax.shift_right_logical(si, shift_blk)
    diag_f = same_blk & (si <= ti)
    diag_b = same_blk & (si >= ti)
    parent_masks = [
        None if 2 * m == C else (lax.shift_right_logical(ti, (2 * m).bit_length() - 1)
                                 == lax.shift_right_logical(si, (2 * m).bit_length() - 1))
        for m in _gla_levels()]

    g_cat = jnp.concatenate([gf_all, gb_all], axis=1)
    prefix = [_cumsum_rows(g_cat[c * C:(c + 1) * C, :], tri) for c in range(n_c)]

    def prep(c):
        rs = slice(c * C, (c + 1) * C)
        q = q_all[rs, :]
        kf = kf_all[rs, :]
        kb = kb_all[rs, :]
        gb = gb_all[rs, :]
        cs = prefix[c]
        bf = cs[:, :B_W]
        cb = cs[:, B_W:]
        eb = cb - gb
        diag_f_ops, lev_f = _score_operands(q, kf, bf, forward=True)
        diag_b_ops, lev_b = _score_operands(q, kb, eb, forward=False)
        bf_end = bf[C - 1:C, :]
        gb_tot = cb[C - 1:C, :]
        qb_ref[rs, :] = (q * jnp.exp(gb_tot - eb)).astype(BF16)
        kb_ref[rs, :] = (kb * jnp.exp(eb)).astype(BF16)
        dg_ref[c:c + 1, :] = jnp.exp(gb_tot)
        return dict(
            rs=rs, v=v_all[rs, :], diag_f=diag_f_ops, diag_b=diag_b_ops, lev_f=lev_f, lev_b=lev_b,
            q_in=(q * jnp.exp(bf)).astype(BF16),
            k_out=(kf * jnp.exp(bf_end - bf)).astype(BF16),
            dec_f=jnp.exp(bf_end))

    def raw_scores(p):
        out = []
        for hd in range(N_HEADS):
            hs = slice(hd * GROUP_W, (hd + 1) * GROUP_W)
            lv = [_dot_nt(*_level_operands(lf[:, hs], lb[:, hs], m))
                  for m, lf, lb in zip(_gla_levels(), p["lev_f"], p["lev_b"])]
            out.append((lv,
                        _dot_nt(p["diag_f"][0][:, hs], p["diag_f"][1][:, hs]),
                        _dot_nt(p["diag_b"][0][:, hs], p["diag_b"][1][:, hs])))
        return out

    def finish(p, raw):
        rs, v = p["rs"], p["v"]
        for hd in range(N_HEADS):
            hs = slice(hd * GROUP_W, (hd + 1) * GROUP_W)
            lv, df, db = raw[hd]
            scores = jnp.where(diag_f, df, 0.0) + jnp.where(diag_b, db, 0.0)
            for a, mk in zip(lv, parent_masks):
                scores = scores + (a if mk is None else jnp.where(mk, a, 0.0))
            st = state[hd]
            op_ref[rs, hs] = (_dot(scores.astype(BF16), v[:, hs])
                              + _dot_nt(p["q_in"][:, hs], st.astype(BF16)))
            state[hd] = p["dec_f"][:, hs] * st + _dot_tn(v[:, hs], p["k_out"][:, hs])

    ready = {c: prep(c) for c in range(min(PREP_LEAD, n_c))}
    raw = {0: raw_scores(ready[0])}
    for c in range(n_c):
        if c + 1 < n_c:
            raw[c + 1] = raw_scores(ready[c + 1])
        if c + PREP_LEAD < n_c:
            ready[c + PREP_LEAD] = prep(c + PREP_LEAD)
        finish(ready.pop(c), raw.pop(c))

    for hd in range(N_HEADS):
        st_ref[hd] = state[hd]

    @pl.when(t_idx == pl.num_programs(1) - 1)
    def _():
        for hd in range(N_HEADS):
            sf_ref[hd] = state[hd]


def _mix1(x, mods, mod_row, gn, w_in, ws, bs, gv, lbf, lbb, s0, *, layer, tm):
    B, L, _ = x.shape
    n_t = L // tm
    n_c = tm // GLA_CHUNK
    tok = lambda w: pl.BlockSpec((None, tm, w), lambda b, t: (b, t, 0))
    state = pl.BlockSpec((None, N_HEADS, GROUP_W, GROUP_W), lambda b, t: (b, 0, 0, 0))
    half = jax.ShapeDtypeStruct((B, L, B_W), BF16)
    return pl.pallas_call(
        functools.partial(_mix1_kernel, tm=tm),
        grid=(B, n_t),
        in_specs=[
            tok(D_MODEL),
            _mod_spec(layer, mod_row),
            _layer_spec(layer, (1, D_MODEL)),
            _layer_spec(layer, (D_MODEL, IN_W)),
            _layer_spec(layer, (N_GROUPS, GMLP_CHUNK, GMLP_CHUNK)),
            _layer_spec(layer, (N_GROUPS, GMLP_CHUNK, GROUP_W)),
            _layer_spec(layer, (1, A_W)),
            _layer_spec(layer, (1, B_W)),
            _layer_spec(layer, (1, B_W)),
            state,
        ],
        out_specs=[
            tok(A_W), tok(B_W), tok(B_W), tok(B_W), tok(B_W), tok(B_W),
            pl.BlockSpec((None, None, n_c, B_W), lambda b, t: (b, t, 0, 0)),
            state,
        ],
        out_shape=[
            half, half, jax.ShapeDtypeStruct((B, L, B_W), F32), half, half, half,
            jax.ShapeDtypeStruct((B, n_t, n_c, B_W), F32),
            jax.ShapeDtypeStruct((B, N_HEADS, GROUP_W, GROUP_W), F32),
        ],
        scratch_shapes=[pltpu.VMEM((N_HEADS, GROUP_W, GROUP_W), F32)],
        compiler_params=pltpu.CompilerParams(
            dimension_semantics=("arbitrary", "arbitrary"), vmem_limit_bytes=VMEM_LIMIT),
        name="mixer_forward",
    )(x, mods, gn.reshape(DEPTH, 1, D_MODEL), w_in, ws, bs, gv.reshape(DEPTH, 1, A_W),
      lbf.reshape(DEPTH, 1, B_W), lbb.reshape(DEPTH, 1, B_W), s0)


def _mix2_kernel(x_ref, mod_ref, a_ref, og_ref, op_ref, qb_ref, kb_ref, v_ref, dg_ref, go_ref,
                 wout_ref, s0_ref, o_ref, sb_ref, st_ref, *, tm):
    C = GLA_CHUNK
    n_c = tm // C
    t_idx = pl.program_id(1)

    @pl.when(t_idx == 0)
    def _():
        st_ref[...] = s0_ref[...]

    state = [st_ref[hd] for hd in range(N_HEADS)]
    o_rows = [None] * n_c
    for c in reversed(range(n_c)):
        rs = slice(c * C, (c + 1) * C)
        qb = qb_ref[rs, :]
        kb = kb_ref[rs, :]
        v = v_ref[rs, :]
        dg = dg_ref[c:c + 1, :]
        o_heads = []
        for hd in range(N_HEADS):
            hs = slice(hd * GROUP_W, (hd + 1) * GROUP_W)
            st = state[hd]
            o_heads.append(op_ref[rs, hs] + _dot_nt(qb[:, hs], st.astype(BF16)))
            state[hd] = dg[:, hs] * st + _dot_tn(v[:, hs], kb[:, hs])
        o_rows[c] = o_heads

    for hd in range(N_HEADS):
        st_ref[hd] = state[hd]

    @pl.when(t_idx == pl.num_programs(1) - 1)
    def _():
        for hd in range(N_HEADS):
            sb_ref[hd] = state[hd]

    y = _dot(a_ref[...], wout_ref[0:A_W, :])
    for hd in range(N_HEADS):
        hs = slice(hd * GROUP_W, (hd + 1) * GROUP_W)
        o_h = jnp.concatenate([o_rows[c][hd] for c in range(n_c)], axis=0)
        on = _rms(o_h, go_ref[:, hs]) * og_ref[:, hs].astype(F32)
        y = y + _dot(on.astype(BF16), wout_ref[A_W + hd * GROUP_W:A_W + (hd + 1) * GROUP_W, :])
    o_ref[...] = x_ref[...] + mod_ref[5:6, :] * y


def _mix2(x, mods, mod_row, a, og, op, qb, kb, v, dg, go, w_out, s0, *, layer, tm):
    B, L, _ = x.shape
    n_t = L // tm
    n_c = tm // GLA_CHUNK
    tok = lambda w: pl.BlockSpec((None, tm, w), lambda b, t: (b, n_t - 1 - t, 0))
    state = pl.BlockSpec((None, N_HEADS, GROUP_W, GROUP_W), lambda b, t: (b, 0, 0, 0))
    return pl.pallas_call(
        functools.partial(_mix2_kernel, tm=tm),
        grid=(B, n_t),
        in_specs=[
            tok(D_MODEL),
            _mod_spec(layer, mod_row),
            tok(A_W), tok(B_W), tok(B_W), tok(B_W), tok(B_W), tok(B_W),
            pl.BlockSpec((None, None, n_c, B_W), lambda b, t: (b, n_t - 1 - t, 0, 0)),
            _layer_spec(layer, (1, B_W)),
            _layer_spec(layer, (D_MODEL, D_MODEL)),
            state,
        ],
        out_specs=[tok(D_MODEL), state],
        out_shape=[
            jax.ShapeDtypeStruct(x.shape, F32),
            jax.ShapeDtypeStruct((B, N_HEADS, GROUP_W, GROUP_W), F32),
        ],
        scratch_shapes=[pltpu.VMEM((N_HEADS, GROUP_W, GROUP_W), F32)],
        compiler_params=pltpu.CompilerParams(
            dimension_semantics=("arbitrary", "arbitrary"), vmem_limit_bytes=VMEM_LIMIT),
        name="mixer_backward",
    )(x, mods, a, og, op, qb, kb, v, dg, go.reshape(DEPTH, 1, B_W), w_out, s0)


def _grid_pos_embed(rows):
    row = jnp.repeat(jnp.arange(rows), GRID_W).astype(F32)
    col = jnp.tile(jnp.arange(GRID_W), rows).astype(F32)
    quarter = D_MODEL // 4
    freq = POS_THETA ** (-jnp.arange(quarter, dtype=F32) / quarter)

    def axis_embed(p):
        a = p[:, None] * freq[None, :]
        return jnp.concatenate([jnp.sin(a), jnp.cos(a)], axis=-1)

    return jnp.concatenate([axis_embed(row), axis_embed(col)], axis=-1)


def _lower_bounds(p):
    cum = jnp.cumsum(jax.nn.softmax(p.astype(F32), axis=0), axis=0)
    return cum - cum[0:1]


def _tile(L, want):
    tm = min(L, want)
    assert L % tm == 0 and tm % GMLP_CHUNK == 0 and tm % GLA_CHUNK == 0
    return tm


def kernel(x, c, ctx, c_ctx, w_ada, b_ada, norm_ffn1_g, norm_mix_g, norm_ffn2_g, ffn1_w1, ffn1_w3,
           ffn1_w2, ffn2_w1, ffn2_w3, ffn2_w2, w_in, w_out, gmlp_ws, gmlp_bs, gmlp_norm_g,
           hgrn_lb_fwd, hgrn_lb_bwd, hgrn_norm_g, norm_final_g):
    B, L, _ = x.shape
    n_ctx = ctx.shape[1]
    assert B + 1 <= MOD_ROWS
    tm_lat = _tile(L, 512)
    tm_ffn = _tile(L, 1024)
    tm_ctx = _tile(n_ctx, 512)

    cond = jnp.concatenate([c, c_ctx[None], jnp.zeros((MOD_ROWS - B - 1, D_MODEL), F32)], axis=0)
    mods = _modulation(cond, w_ada, b_ada).reshape(DEPTH, MOD_ROWS, N_MOD, D_MODEL)
    lat_row = lambda b: b
    ctx_row = lambda b: B

    pos = _grid_pos_embed(L // GRID_W)
    lbs_f = _lower_bounds(hgrn_lb_fwd)
    lbs_b = _lower_bounds(hgrn_lb_bwd)
    bs_full = jnp.broadcast_to(gmlp_bs[..., None], gmlp_bs.shape + (GROUP_W,))
    ffn1 = [w.astype(BF16) for w in (ffn1_w1, ffn1_w3, ffn1_w2)]
    ffn2 = [w.astype(BF16) for w in (ffn2_w1, ffn2_w3, ffn2_w2)]
    zero_state = jnp.zeros((B, N_HEADS, GROUP_W, GROUP_W), F32)
    mix1 = functools.partial(_mix1, gn=norm_mix_g, w_in=w_in.astype(BF16), ws=gmlp_ws.astype(BF16),
                             bs=bs_full, gv=gmlp_norm_g, lbf=lbs_f, lbb=lbs_b)
    mix2 = functools.partial(_mix2, go=hgrn_norm_g, w_out=w_out.astype(BF16))

    lat, cx = x, ctx
    for l in range(DEPTH):
        last = l == DEPTH - 1
        lat = _ffn(lat, mods, lat_row, norm_ffn1_g, *ffn1, layer=l, j0=0, tm=tm_ffn,
                   pos=pos if l == 0 else None)
        cx = _ffn(cx, mods, ctx_row, norm_ffn1_g, *ffn1, layer=l, j0=0, tm=tm_ctx)

        *parts_c, s_f = mix1(cx, mods, ctx_row, s0=zero_state, layer=l, tm=tm_ctx)
        cx_mixed, s_b = mix2(cx, mods, ctx_row, *parts_c, s0=zero_state, layer=l, tm=tm_ctx)
        *parts_l, _ = mix1(lat, mods, lat_row, s0=s_f, layer=l, tm=tm_lat)
        lat, _ = mix2(lat, mods, lat_row, *parts_l, s0=s_b, layer=l, tm=tm_lat)

        lat = _ffn(lat, mods, lat_row, norm_ffn2_g, *ffn2, layer=l, j0=6, tm=tm_ffn,
                   gfin=norm_final_g if last else None)
        if not last:
            cx = _ffn(cx_mixed, mods, ctx_row, norm_ffn2_g, *ffn2, layer=l, j0=6, tm=tm_ctx)
    return lat
```

```python
import functools

import jax
import jax.numpy as jnp
from jax import lax
from jax.experimental import pallas as pl
from jax.experimental.pallas import tpu as pltpu

D_MODEL = 1024
D_FF = 2816
DEPTH = 4
GRID_W = 64
A_W = 512
B_W = 512
IN_W = 2 * A_W + 5 * B_W
N_GROUPS = 4
GROUP_W = 128
GMLP_CHUNK = 128
N_HEADS = 4
N_MOD = 9
MOD_ROWS = 16
EPS = 1e-6
POS_THETA = 10000.0

GLA_CHUNK = 64
GLA_BLOCK = 16
LOG_F_MIN = -10.0
PREP_LEAD = 2
FF_SPLITS = ((0, 768), (768, 768), (1536, 768), (2304, 512))

V7X_VMEM_BYTES = 64 * 1024 * 1024
VMEM_LIMIT = V7X_VMEM_BYTES - 8 * 1024 * 1024

F32 = jnp.float32
BF16 = jnp.bfloat16


def _silu(x):
    hx = 0.5 * x
    return hx + hx * jnp.tanh(hx)


def _gelu_tanh(x):
    hx = 0.5 * x
    return hx + hx * jnp.tanh(x * (0.7978845608028654 + 0.035677408136300125 * (x * x)))


def _rms(x, g):
    return x * lax.rsqrt(jnp.mean(x * x, axis=-1, keepdims=True) + EPS) * g


def _dot(a, b):
    return jnp.dot(a, b, preferred_element_type=F32)


def _dot_nt(a, b):
    return lax.dot_general(a, b, (((1,), (1,)), ((), ())), preferred_element_type=F32)


def _dot_tn(a, b):
    return lax.dot_general(a, b, (((0,), (0,)), ((), ())), preferred_element_type=F32)


def _layer_spec(layer, shape):
    nd = len(shape)
    return pl.BlockSpec((None,) + shape, lambda *_: (layer,) + (0,) * nd, pipeline_mode=pl.Buffered(1))


def _mod_spec(layer, mod_row):
    return pl.BlockSpec((None, None, N_MOD, D_MODEL), lambda b, t: (layer, mod_row(b), 0, 0))


def _mod_kernel(cond_ref, w_ref, b_ref, o_ref):
    s = _silu(cond_ref[...])
    o_ref[...] = jnp.dot(s, w_ref[...], preferred_element_type=F32,
                         precision=lax.Precision.HIGHEST) + b_ref[...]


def _modulation(cond, w_ada, b_ada):
    n_out = N_MOD * D_MODEL
    return pl.pallas_call(
        _mod_kernel,
        grid=(DEPTH, N_MOD),
        in_specs=[
            pl.BlockSpec((MOD_ROWS, D_MODEL), lambda l, j: (0, 0)),
            pl.BlockSpec((None, D_MODEL, D_MODEL), lambda l, j: (l, 0, j)),
            pl.BlockSpec((None, 1, D_MODEL), lambda l, j: (l, 0, j)),
        ],
        out_specs=pl.BlockSpec((None, MOD_ROWS, D_MODEL), lambda l, j: (l, 0, j)),
        out_shape=jax.ShapeDtypeStruct((DEPTH, MOD_ROWS, n_out), F32),
        compiler_params=pltpu.CompilerParams(dimension_semantics=("arbitrary", "arbitrary")),
        name="adaln_modulation",
    )(cond, w_ada, b_ada.reshape(DEPTH, 1, n_out))


def _ffn_kernel(*refs, j0, add_pos, final):
    refs = list(refs)
    x_ref = refs.pop(0)
    pos_ref = refs.pop(0) if add_pos else None
    mod_ref, g_ref, w1_ref, w3_ref, w2_ref = refs[:5]
    refs = refs[5:]
    gfin_ref = refs.pop(0) if final else None
    o_ref = refs.pop(0)

    x = x_ref[...]
    if add_pos:
        x = x + pos_ref[...]
    shift = mod_ref[j0:j0 + 1, :]
    scale = mod_ref[j0 + 1:j0 + 2, :]
    gate = mod_ref[j0 + 2:j0 + 3, :]
    h = (_rms(x, g_ref[...]) * (1.0 + scale) + shift).astype(BF16)
    y = None
    for s, n in FF_SPLITS:
        a = _dot(h, w1_ref[:, s:s + n])
        b = _dot(h, w3_ref[:, s:s + n])
        p = (_silu(a) * b).astype(BF16)
        yc = _dot(p, w2_ref[s:s + n, :])
        y = yc if y is None else y + yc
    out = x + (0.5 * gate) * y
    if final:
        out = _rms(out, gfin_ref[...])
    o_ref[...] = out


def _ffn(x, mods, mod_row, g, w1, w3, w2, *, layer, j0, tm, pos=None, gfin=None):
    B, L, _ = x.shape
    add_pos, final = pos is not None, gfin is not None
    tok = pl.BlockSpec((None, tm, D_MODEL), lambda b, t: (b, t, 0))
    in_specs, args = [tok], [x]
    if add_pos:
        in_specs.append(pl.BlockSpec((tm, D_MODEL), lambda b, t: (t, 0)))
        args.append(pos)
    in_specs += [
        _mod_spec(layer, mod_row),
        _layer_spec(layer, (1, D_MODEL)),
        _layer_spec(layer, (D_MODEL, D_FF)),
        _layer_spec(layer, (D_MODEL, D_FF)),
        _layer_spec(layer, (D_FF, D_MODEL)),
    ]
    args += [mods, g.reshape(DEPTH, 1, D_MODEL), w1, w3, w2]
    if final:
        in_specs.append(pl.BlockSpec((1, D_MODEL), lambda b, t: (0, 0)))
        args.append(gfin.reshape(1, D_MODEL))
    return pl.pallas_call(
        functools.partial(_ffn_kernel, j0=j0, add_pos=add_pos, final=final),
        grid=(B, L // tm),
        in_specs=in_specs,
        out_specs=tok,
        out_shape=jax.ShapeDtypeStruct(x.shape, F32),
        compiler_params=pltpu.CompilerParams(
            dimension_semantics=("parallel", "parallel"), vmem_limit_bytes=VMEM_LIMIT),
        name="adaln_swiglu",
    )(*args)


def _forget(f_raw, lb):
    t = jnp.tanh(0.5 * f_raw)
    half = 0.5 * (1.0 - lb)
    k = half - half * t
    log_f = jnp.maximum(jnp.log((lb + half) + half * t), LOG_F_MIN)
    return k, log_f


def _cumsum_rows(g, tri3):
    hi = g.astype(BF16)
    r1 = g - hi.astype(F32)
    mid = r1.astype(BF16)
    lo = (r1 - mid.astype(F32)).astype(BF16)
    return _dot(tri3, jnp.concatenate([hi, mid, lo], axis=0))


def _ref_rows(a, first, period):
    n = a.shape[0] // period
    parts = [jnp.broadcast_to(a[first + j * period:first + j * period + 1, :], (period, a.shape[1]))
             for j in range(n)]
    return parts[0] if n == 1 else jnp.concatenate(parts, axis=0)


def _gla_levels():
    levels = []
    m = GLA_BLOCK
    while m < GLA_CHUNK:
        levels.append(m)
        m *= 2
    return tuple(levels)


def _interleave_rows(first, second, m):
    C = first.shape[0]
    parts = [(first if (i // m) % 2 == 0 else second)[i:i + m] for i in range(0, C, m)]
    return jnp.concatenate(parts, axis=0)


def _score_operands(q, k, b, *, forward):
    half = GLA_BLOCK // 2
    d = b - _ref_rows(b, half - 1 if forward else half, GLA_BLOCK)
    if not forward:
        d = -d
    diag = ((q * jnp.exp(d)).astype(BF16), (k * jnp.exp(-d)).astype(BF16))
    levels = []
    for m in _gla_levels():
        r = _ref_rows(b, m - 1 if forward else m, 2 * m)
        qk = _interleave_rows(k, q, m) if forward else _interleave_rows(q, k, m)
        levels.append((qk * jnp.exp(-jnp.abs(b - r))).astype(BF16))
    return diag, levels


def _level_operands(lf, lb, m):
    zf, zb = jnp.zeros_like(lf), jnp.zeros_like(lb)
    q_side = jnp.concatenate([_interleave_rows(zf, lf, m), _interleave_rows(lb, zb, m)], axis=1)
    k_side = jnp.concatenate([_interleave_rows(lf, zf, m), _interleave_rows(zb, lb, m)], axis=1)
    return q_side, k_side


def _mix1_kernel(x_ref, mod_ref, gn_ref, win_ref, ws_ref, bs_ref, gv_ref, lbf_ref, lbb_ref, s0_ref,
                 a_ref, og_ref, op_ref, qb_ref, ds_ref, dg_ref, sf_ref, st_ref, *, tm):
    C = GLA_CHUNK
    n_c = tm // C
    t_idx = pl.program_id(1)

    @pl.when(t_idx == 0)
    def _():
        st_ref[...] = s0_ref[...]

    state = [st_ref[hd] for hd in range(N_HEADS)]

    x = x_ref[...]
    shift = mod_ref[3:4, :]
    scale = mod_ref[4:5, :]
    h = (_rms(x, gn_ref[...]) * (1.0 + scale) + shift).astype(BF16)
    z = _dot(h, win_ref[...])

    n_g = tm // GMLP_CHUNK
    for g in range(N_GROUPS):
        cu = slice(g * GROUP_W, (g + 1) * GROUP_W)
        cv = slice(A_W + g * GROUP_W, A_W + (g + 1) * GROUP_W)
        vn = _rms(_gelu_tanh(z[:, cv]), gv_ref[:, cu]).astype(BF16)
        vn_wide = jnp.concatenate(
            [vn[cc * GMLP_CHUNK:(cc + 1) * GMLP_CHUNK, :] for cc in range(n_g)], axis=1)
        sv_wide = _dot(ws_ref[g], vn_wide)
        for cc in range(n_g):
            rows = slice(cc * GMLP_CHUNK, (cc + 1) * GMLP_CHUNK)
            sv = sv_wide[:, cc * GROUP_W:(cc + 1) * GROUP_W] + bs_ref[g]
            a_ref[rows, cu] = (_gelu_tanh(z[rows, cu]) * sv).astype(BF16)

    c0 = 2 * A_W
    q_all = _silu(z[:, c0:c0 + B_W])
    kf_all, gf_all = _forget(z[:, c0 + B_W:c0 + 2 * B_W], lbf_ref[...])
    kb_all, gb_all = _forget(z[:, c0 + 2 * B_W:c0 + 3 * B_W], lbb_ref[...])
    v_all = z[:, c0 + 3 * B_W:c0 + 4 * B_W].astype(BF16)
    og_ref[...] = _silu(z[:, c0 + 4 * B_W:c0 + 5 * B_W]).astype(BF16)

    ti = lax.broadcasted_iota(jnp.int32, (C, C), 0)
    si = lax.broadcasted_iota(jnp.int32, (C, C), 1)
    tri = (si <= ti).astype(BF16)
    tri3 = jnp.concatenate([tri, tri, tri], axis=1)
    shift_blk = GLA_BLOCK.bit_length() - 1
    same_blk = lax.shift_right_logical(ti, shift_blk) == lax.shift_right_logical(si, shift_blk)
    diag_f = same_blk & (si <= ti)
    diag_b = same_blk & (si >= ti)
    parent_masks = [
        None if 2 * m == C else (lax.shift_right_logical(ti, (2 * m).bit_length() - 1)
                                 == lax.shift_right_logical(si, (2 * m).bit_length() - 1))
        for m in _gla_levels()]

    g_cat = jnp.concatenate([gf_all, gb_all], axis=1)
    prefix = [_cumsum_rows(g_cat[c * C:(c + 1) * C, :], tri3) for c in range(n_c)]

    def prep(c):
        rs = slice(c * C, (c + 1) * C)
        q = q_all[rs, :]
        kf = kf_all[rs, :]
        kb = kb_all[rs, :]
        gb = gb_all[rs, :]
        cs = prefix[c]
        bf = cs[:, :B_W]
        cb = cs[:, B_W:]
        eb = cb - gb
        diag_f_ops, lev_f = _score_operands(q, kf, bf, forward=True)
        diag_b_ops, lev_b = _score_operands(q, kb, eb, forward=False)
        bf_end = bf[C - 1:C, :]
        gb_tot = cb[C - 1:C, :]
        qb_ref[rs, :] = (q * jnp.exp(gb_tot - eb)).astype(BF16)
        dg_ref[c:c + 1, :] = jnp.exp(gb_tot)
        return dict(
            c=c, rs=rs, v=v_all[rs, :], diag_f=diag_f_ops, diag_b=diag_b_ops, lev_f=lev_f, lev_b=lev_b,
            q_in=(q * jnp.exp(bf)).astype(BF16),
            k_out_f=(kf * jnp.exp(bf_end - bf)).astype(BF16),
            k_out_b=(kb * jnp.exp(eb)).astype(BF16),
            dec_f=jnp.exp(bf_end))

    def raw_scores(p):
        out = []
        for hd in range(N_HEADS):
            hs = slice(hd * GROUP_W, (hd + 1) * GROUP_W)
            lv = [_dot_nt(*_level_operands(lf[:, hs], lb[:, hs], m))
                  for m, lf, lb in zip(_gla_levels(), p["lev_f"], p["lev_b"])]
            out.append((lv,
                        _dot_nt(p["diag_f"][0][:, hs], p["diag_f"][1][:, hs]),
                        _dot_nt(p["diag_b"][0][:, hs], p["diag_b"][1][:, hs])))
        return out

    def finish(p, raw):
        rs, v = p["rs"], p["v"]
        for hd in range(N_HEADS):
            hs = slice(hd * GROUP_W, (hd + 1) * GROUP_W)
            lv, df, db = raw[hd]
            scores = jnp.where(diag_f, df, 0.0) + jnp.where(diag_b, db, 0.0)
            for a, mk in zip(lv, parent_masks):
                scores = scores + (a if mk is None else jnp.where(mk, a, 0.0))
            st = state[hd]
            op_ref[rs, hs] = (_dot(scores.astype(BF16), v[:, hs])
                              + _dot_nt(p["q_in"][:, hs], st.astype(BF16)))
            inc = _dot_tn(v[:, hs], jnp.concatenate([p["k_out_f"][:, hs], p["k_out_b"][:, hs]], axis=1))
            state[hd] = p["dec_f"][:, hs] * st + inc[:, :GROUP_W]
            ds_ref[p["c"], hd] = inc[:, GROUP_W:]

    ready = {c: prep(c) for c in range(min(PREP_LEAD, n_c))}
    raw = {0: raw_scores(ready[0])}
    for c in range(n_c):
        if c + 1 < n_c:
            raw[c + 1] = raw_scores(ready[c + 1])
        if c + PREP_LEAD < n_c:
            ready[c + PREP_LEAD] = prep(c + PREP_LEAD)
        finish(ready.pop(c), raw.pop(c))

    for hd in range(N_HEADS):
        st_ref[hd] = state[hd]

    @pl.when(t_idx == pl.num_programs(1) - 1)
    def _():
        for hd in range(N_HEADS):
            sf_ref[hd] = state[hd]


def _mix1(x, mods, mod_row, gn, w_in, ws, bs, gv, lbf, lbb, s0, *, layer, tm):
    B, L, _ = x.shape
    n_t = L // tm
    n_c = tm // GLA_CHUNK
    tok = lambda w: pl.BlockSpec((None, tm, w), lambda b, t: (b, t, 0))
    state = pl.BlockSpec((None, N_HEADS, GROUP_W, GROUP_W), lambda b, t: (b, 0, 0, 0))
    half = jax.ShapeDtypeStruct((B, L, B_W), BF16)
    return pl.pallas_call(
        functools.partial(_mix1_kernel, tm=tm),
        grid=(B, n_t),
        in_specs=[
            tok(D_MODEL),
            _mod_spec(layer, mod_row),
            _layer_spec(layer, (1, D_MODEL)),
            _layer_spec(layer, (D_MODEL, IN_W)),
            _layer_spec(layer, (N_GROUPS, GMLP_CHUNK, GMLP_CHUNK)),
            _layer_spec(layer, (N_GROUPS, GMLP_CHUNK, GROUP_W)),
            _layer_spec(layer, (1, A_W)),
            _layer_spec(layer, (1, B_W)),
            _layer_spec(layer, (1, B_W)),
            state,
        ],
        out_specs=[
            tok(A_W), tok(B_W), tok(B_W), tok(B_W),
            pl.BlockSpec((None, n_c, N_HEADS, GROUP_W, GROUP_W), lambda b, t: (b, t, 0, 0, 0)),
            pl.BlockSpec((None, None, n_c, B_W), lambda b, t: (b, t, 0, 0)),
            state,
        ],
        out_shape=[
            half, half, jax.ShapeDtypeStruct((B, L, B_W), F32), half,
            jax.ShapeDtypeStruct((B, L // GLA_CHUNK, N_HEADS, GROUP_W, GROUP_W), F32),
            jax.ShapeDtypeStruct((B, n_t, n_c, B_W), F32),
            jax.ShapeDtypeStruct((B, N_HEADS, GROUP_W, GROUP_W), F32),
        ],
        scratch_shapes=[pltpu.VMEM((N_HEADS, GROUP_W, GROUP_W), F32)],
        compiler_params=pltpu.CompilerParams(
            dimension_semantics=("arbitrary", "arbitrary"), vmem_limit_bytes=VMEM_LIMIT),
        name="mixer_forward",
    )(x, mods, gn.reshape(DEPTH, 1, D_MODEL), w_in, ws, bs, gv.reshape(DEPTH, 1, A_W),
      lbf.reshape(DEPTH, 1, B_W), lbb.reshape(DEPTH, 1, B_W), s0)


def _mix2_kernel(x_ref, mod_ref, a_ref, og_ref, op_ref, qb_ref, ds_ref, dg_ref, go_ref,
                 wout_ref, s0_ref, o_ref, sb_ref, st_ref, *, tm):
    C = GLA_CHUNK
    n_c = tm // C
    t_idx = pl.program_id(1)

    @pl.when(t_idx == 0)
    def _():
        st_ref[...] = s0_ref[...]

    state = [st_ref[hd] for hd in range(N_HEADS)]
    o_rows = [None] * n_c
    for c in reversed(range(n_c)):
        rs = slice(c * C, (c + 1) * C)
        qb = qb_ref[rs, :]
        dg = dg_ref[c:c + 1, :]
        o_heads = []
        for hd in range(N_HEADS):
            hs = slice(hd * GROUP_W, (hd + 1) * GROUP_W)
            st = state[hd]
            o_heads.append(op_ref[rs, hs] + _dot_nt(qb[:, hs], st.astype(BF16)))
            state[hd] = dg[:, hs] * st + ds_ref[c, hd]
        o_rows[c] = o_heads

    for hd in range(N_HEADS):
        st_ref[hd] = state[hd]

    @pl.when(t_idx == pl.num_programs(1) - 1)
    def _():
        for hd in range(N_HEADS):
            sb_ref[hd] = state[hd]

    y = _dot(a_ref[...], wout_ref[0:A_W, :])
    for hd in range(N_HEADS):
        hs = slice(hd * GROUP_W, (hd + 1) * GROUP_W)
        o_h = jnp.concatenate([o_rows[c][hd] for c in range(n_c)], axis=0)
        on = _rms(o_h, go_ref[:, hs]) * og_ref[:, hs].astype(F32)
        y = y + _dot(on.astype(BF16), wout_ref[A_W + hd * GROUP_W:A_W + (hd + 1) * GROUP_W, :])
    o_ref[...] = x_ref[...] + mod_ref[5:6, :] * y


def _mix2(x, mods, mod_row, a, og, op, qb, ds, dg, go, w_out, s0, *, layer, tm):
    B, L, _ = x.shape
    n_t = L // tm
    n_c = tm // GLA_CHUNK
    tok = lambda w: pl.BlockSpec((None, tm, w), lambda b, t: (b, n_t - 1 - t, 0))
    state = pl.BlockSpec((None, N_HEADS, GROUP_W, GROUP_W), lambda b, t: (b, 0, 0, 0))
    return pl.pallas_call(
        functools.partial(_mix2_kernel, tm=tm),
        grid=(B, n_t),
        in_specs=[
            tok(D_MODEL),
            _mod_spec(layer, mod_row),
            tok(A_W), tok(B_W), tok(B_W), tok(B_W),
            pl.BlockSpec((None, n_c, N_HEADS, GROUP_W, GROUP_W), lambda b, t: (b, n_t - 1 - t, 0, 0, 0)),
            pl.BlockSpec((None, None, n_c, B_W), lambda b, t: (b, n_t - 1 - t, 0, 0)),
            _layer_spec(layer, (1, B_W)),
            _layer_spec(layer, (D_MODEL, D_MODEL)),
            state,
        ],
        out_specs=[tok(D_MODEL), state],
        out_shape=[
            jax.ShapeDtypeStruct(x.shape, F32),
            jax.ShapeDtypeStruct((B, N_HEADS, GROUP_W, GROUP_W), F32),
        ],
        scratch_shapes=[pltpu.VMEM((N_HEADS, GROUP_W, GROUP_W), F32)],
        compiler_params=pltpu.CompilerParams(
            dimension_semantics=("arbitrary", "arbitrary"), vmem_limit_bytes=VMEM_LIMIT),
        name="mixer_backward",
    )(x, mods, a, og, op, qb, ds, dg, go.reshape(DEPTH, 1, B_W), w_out, s0)


def _grid_pos_embed(rows):
    quarter = D_MODEL // 4
    freq = POS_THETA ** (-jnp.arange(quarter, dtype=F32) / quarter)

    def axis_embed(n):
        a = jnp.arange(n).astype(F32)[:, None] * freq[None, :]
        return jnp.concatenate([jnp.sin(a), jnp.cos(a)], axis=-1)

    row_e = jnp.repeat(axis_embed(rows), GRID_W, axis=0)
    col_e = jnp.tile(axis_embed(GRID_W), (rows, 1))
    return jnp.concatenate([row_e, col_e], axis=-1)


def _lower_bounds(p):
    cum = jnp.cumsum(jax.nn.softmax(p.astype(F32), axis=0), axis=0)
    return cum - cum[0:1]


def _tile(L, want):
    tm = min(L, want)
    assert L % tm == 0 and tm % GMLP_CHUNK == 0 and tm % GLA_CHUNK == 0
    return tm


def kernel(x, c, ctx, c_ctx, w_ada, b_ada, norm_ffn1_g, norm_mix_g, norm_ffn2_g, ffn1_w1, ffn1_w3,
           ffn1_w2, ffn2_w1, ffn2_w3, ffn2_w2, w_in, w_out, gmlp_ws, gmlp_bs, gmlp_norm_g,
           hgrn_lb_fwd, hgrn_lb_bwd, hgrn_norm_g, norm_final_g):
    B, L, _ = x.shape
    n_ctx = ctx.shape[1]
    assert B + 1 <= MOD_ROWS
    tm_lat = _tile(L, 512)
    tm_mix2 = _tile(L, 1024)
    tm_ffn = _tile(L, 1024)
    tm_ctx = _tile(n_ctx, 512)

    cond = jnp.concatenate([c, c_ctx[None], jnp.zeros((MOD_ROWS - B - 1, D_MODEL), F32)], axis=0)
    mods = _modulation(cond, w_ada, b_ada).reshape(DEPTH, MOD_ROWS, N_MOD, D_MODEL)
    lat_row = lambda b: b
    ctx_row = lambda b: B

    pos = _grid_pos_embed(L // GRID_W)
    lbs_f = _lower_bounds(hgrn_lb_fwd)
    lbs_b = _lower_bounds(hgrn_lb_bwd)
    bs_full = jnp.broadcast_to(gmlp_bs[..., None], gmlp_bs.shape + (GROUP_W,))
    ffn1 = [w.astype(BF16) for w in (ffn1_w1, ffn1_w3, ffn1_w2)]
    ffn2 = [w.astype(BF16) for w in (ffn2_w1, ffn2_w3, ffn2_w2)]
    zero_state = jnp.zeros((B, N_HEADS, GROUP_W, GROUP_W), F32)
    mix1 = functools.partial(_mix1, gn=norm_mix_g, w_in=w_in.astype(BF16), ws=gmlp_ws.astype(BF16),
                             bs=bs_full, gv=gmlp_norm_g, lbf=lbs_f, lbb=lbs_b)
    mix2 = functools.partial(_mix2, go=hgrn_norm_g, w_out=w_out.astype(BF16))

    lat, cx = x, ctx
    for l in range(DEPTH):
        last = l == DEPTH - 1
        lat = _ffn(lat, mods, lat_row, norm_ffn1_g, *ffn1, layer=l, j0=0, tm=tm_ffn,
                   pos=pos if l == 0 else None)
        cx = _ffn(cx, mods, ctx_row, norm_ffn1_g, *ffn1, layer=l, j0=0, tm=tm_ctx)

        *parts_c, s_f = mix1(cx, mods, ctx_row, s0=zero_state, layer=l, tm=tm_ctx)
        cx_mixed, s_b = mix2(cx, mods, ctx_row, *parts_c, s0=zero_state, layer=l, tm=tm_ctx)
        *parts_l, dg_l, _ = mix1(lat, mods, lat_row, s0=s_f, layer=l, tm=tm_lat)
        dg_l = dg_l.reshape(B, L // tm_mix2, tm_mix2 // GLA_CHUNK, B_W)
        lat, _ = mix2(lat, mods, lat_row, *parts_l, dg_l, s0=s_b, layer=l, tm=tm_mix2)

        lat = _ffn(lat, mods, lat_row, norm_ffn2_g, *ffn2, layer=l, j0=6, tm=tm_ffn,
                   gfin=norm_final_g if last else None)
        if not last:
            cx = _ffn(cx_mixed, mods, ctx_row, norm_ffn2_g, *ffn2, layer=l, j0=6, tm=tm_ctx)
    return lat
```

```python
import functools

import jax
import jax.numpy as jnp
from jax import lax
from jax.experimental import pallas as pl
from jax.experimental.pallas import tpu as pltpu

D_MODEL = 1024
D_FF = 2816
DEPTH = 4
GRID_W = 64
A_W = 512
B_W = 512
IN_W = 2 * A_W + 5 * B_W
N_GROUPS = 4
GROUP_W = 128
GMLP_CHUNK = 128
N_HEADS = 4
N_MOD = 9
MOD_ROWS = 16
MOD_COLS = 3 * D_MODEL
EPS = 1e-6
POS_THETA = 10000.0

GLA_CHUNK = 64
GLA_BLOCK = 16
LOG_F_MIN = -10.0
PREP_LEAD = 2
FF_SPLITS = ((0, 768), (768, 768), (1536, 768), (2304, 512))

V7X_VMEM_BYTES = 64 * 1024 * 1024
VMEM_LIMIT = V7X_VMEM_BYTES - 8 * 1024 * 1024

F32 = jnp.float32
BF16 = jnp.bfloat16


def _silu(x):
    hx = 0.5 * x
    return hx + hx * jnp.tanh(hx)


def _gelu_tanh(x):
    hx = 0.5 * x
    return hx + hx * jnp.tanh(x * (0.7978845608028654 + 0.035677408136300125 * (x * x)))


def _rms(x, g):
    return x * lax.rsqrt(jnp.mean(x * x, axis=-1, keepdims=True) + EPS) * g


def _modulated_rms(x, g, shift, scale):
    return x * lax.rsqrt(jnp.mean(x * x, axis=-1, keepdims=True) + EPS) * (g * (1.0 + scale)) + shift


def _dot(a, b):
    return jnp.dot(a, b, preferred_element_type=F32)


def _dot_nt(a, b):
    return lax.dot_general(a, b, (((1,), (1,)), ((), ())), preferred_element_type=F32)


def _dot_tn(a, b):
    return lax.dot_general(a, b, (((0,), (0,)), ((), ())), preferred_element_type=F32)


def _layer_spec(layer, shape):
    nd = len(shape)
    return pl.BlockSpec((None,) + shape, lambda *_: (layer,) + (0,) * nd, pipeline_mode=pl.Buffered(1))


def _mod_spec(layer, mod_row):
    return pl.BlockSpec((None, None, N_MOD, D_MODEL), lambda b, t: (layer, mod_row(b), 0, 0))


def _mod_kernel(cond_ref, w_ref, b_ref, o_ref):
    s = _silu(cond_ref[...])
    o_ref[...] = jnp.dot(s, w_ref[...], preferred_element_type=F32,
                         precision=lax.Precision.HIGHEST) + b_ref[...]


def _modulation(cond, w_ada, b_ada):
    n_out = N_MOD * D_MODEL
    tn = MOD_COLS
    return pl.pallas_call(
        _mod_kernel,
        grid=(DEPTH, n_out // tn),
        in_specs=[
            pl.BlockSpec((MOD_ROWS, D_MODEL), lambda l, j: (0, 0)),
            pl.BlockSpec((None, D_MODEL, tn), lambda l, j: (l, 0, j)),
            pl.BlockSpec((None, 1, tn), lambda l, j: (l, 0, j)),
        ],
        out_specs=pl.BlockSpec((None, MOD_ROWS, tn), lambda l, j: (l, 0, j)),
        out_shape=jax.ShapeDtypeStruct((DEPTH, MOD_ROWS, n_out), F32),
        compiler_params=pltpu.CompilerParams(
            dimension_semantics=("arbitrary", "arbitrary"), vmem_limit_bytes=VMEM_LIMIT),
        name="adaln_modulation",
    )(cond, w_ada, b_ada.reshape(DEPTH, 1, n_out))


def _ffn_kernel(*refs, j0, add_pos, final):
    refs = list(refs)
    x_ref = refs.pop(0)
    pos_ref = refs.pop(0) if add_pos else None
    mod_ref, g_ref, w1_ref, w3_ref, w2_ref = refs[:5]
    refs = refs[5:]
    gfin_ref = refs.pop(0) if final else None
    o_ref = refs.pop(0)

    x = x_ref[...]
    if add_pos:
        x = x + pos_ref[...]
    shift = mod_ref[j0:j0 + 1, :]
    scale = mod_ref[j0 + 1:j0 + 2, :]
    gate = mod_ref[j0 + 2:j0 + 3, :]
    h = _modulated_rms(x, g_ref[...], shift, scale).astype(BF16)
    y = None
    for s, n in FF_SPLITS:
        a = _dot(h, w1_ref[:, s:s + n])
        b = _dot(h, w3_ref[:, s:s + n])
        p = (_silu(a) * b).astype(BF16)
        yc = _dot(p, w2_ref[s:s + n, :])
        y = yc if y is None else y + yc
    out = x + (0.5 * gate) * y
    if final:
        out = _rms(out, gfin_ref[...])
    o_ref[...] = out


def _ffn(x, mods, mod_row, g, w1, w3, w2, *, layer, j0, tm, pos=None, gfin=None):
    B, L, _ = x.shape
    add_pos, final = pos is not None, gfin is not None
    tok = pl.BlockSpec((None, tm, D_MODEL), lambda b, t: (b, t, 0))
    in_specs, args = [tok], [x]
    if add_pos:
        in_specs.append(pl.BlockSpec((tm, D_MODEL), lambda b, t: (t, 0)))
        args.append(pos)
    in_specs += [
        _mod_spec(layer, mod_row),
        _layer_spec(layer, (1, D_MODEL)),
        _layer_spec(layer, (D_MODEL, D_FF)),
        _layer_spec(layer, (D_MODEL, D_FF)),
        _layer_spec(layer, (D_FF, D_MODEL)),
    ]
    args += [mods, g.reshape(DEPTH, 1, D_MODEL), w1, w3, w2]
    if final:
        in_specs.append(pl.BlockSpec((1, D_MODEL), lambda b, t: (0, 0)))
        args.append(gfin.reshape(1, D_MODEL))
    return pl.pallas_call(
        functools.partial(_ffn_kernel, j0=j0, add_pos=add_pos, final=final),
        grid=(B, L // tm),
        in_specs=in_specs,
        out_specs=tok,
        out_shape=jax.ShapeDtypeStruct(x.shape, F32),
        compiler_params=pltpu.CompilerParams(
            dimension_semantics=("parallel", "parallel"), vmem_limit_bytes=VMEM_LIMIT),
        name="adaln_swiglu",
    )(*args)


def _forget(f_raw, lb):
    t = jnp.tanh(0.5 * f_raw)
    half = 0.5 * (1.0 - lb)
    k = half - half * t
    log_f = jnp.maximum(jnp.log((lb + half) + half * t), LOG_F_MIN)
    return k, log_f


def _cumsum_rows(g, tri3):
    hi = g.astype(BF16)
    r1 = g - hi.astype(F32)
    mid = r1.astype(BF16)
    lo = (r1 - mid.astype(F32)).astype(BF16)
    return _dot(tri3, jnp.concatenate([hi, mid, lo], axis=0))


def _ref_rows(a, first, period):
    n = a.shape[0] // period
    parts = [jnp.broadcast_to(a[first + j * period:first + j * period + 1, :], (period, a.shape[1]))
             for j in range(n)]
    return parts[0] if n == 1 else jnp.concatenate(parts, axis=0)


def _gla_levels():
    levels = []
    m = GLA_BLOCK
    while m < GLA_CHUNK:
        levels.append(m)
        m *= 2
    return tuple(levels)


def _interleave_rows(first, second, m):
    C = first.shape[0]
    parts = [(first if (i // m) % 2 == 0 else second)[i:i + m] for i in range(0, C, m)]
    return jnp.concatenate(parts, axis=0)


def _score_operands(q, k, b, *, forward):
    half = GLA_BLOCK // 2
    d = b - _ref_rows(b, half - 1 if forward else half, GLA_BLOCK)
    if not forward:
        d = -d
    diag = ((q * jnp.exp(d)).astype(BF16), (k * jnp.exp(-d)).astype(BF16))
    levels = []
    for m in _gla_levels():
        r = _ref_rows(b, m - 1 if forward else m, 2 * m)
        qk = _interleave_rows(k, q, m) if forward else _interleave_rows(q, k, m)
        levels.append((qk * jnp.exp(-jnp.abs(b - r))).astype(BF16))
    return diag, levels


def _level_operands(lf, lb, m):
    zf, zb = jnp.zeros_like(lf), jnp.zeros_like(lb)
    q_side = jnp.concatenate([_interleave_rows(zf, lf, m), _interleave_rows(lb, zb, m)], axis=1)
    k_side = jnp.concatenate([_interleave_rows(lf, zf, m), _interleave_rows(zb, lb, m)], axis=1)
    return q_side, k_side


def _mix1_kernel(x_ref, mod_ref, gn_ref, win_ref, ws_ref, bs_ref, gv_ref, lbf_ref, lbb_ref, s0_ref,
                 a_ref, og_ref, op_ref, qb_ref, ds_ref, dg_ref, sf_ref, st_ref, *, tm):
    C = GLA_CHUNK
    n_c = tm // C
    t_idx = pl.program_id(1)

    @pl.when(t_idx == 0)
    def _():
        st_ref[...] = s0_ref[...]

    state = [st_ref[hd] for hd in range(N_HEADS)]

    x = x_ref[...]
    shift = mod_ref[3:4, :]
    scale = mod_ref[4:5, :]
    h = _modulated_rms(x, gn_ref[...], shift, scale).astype(BF16)
    z = _dot(h, win_ref[...])

    n_g = tm // GMLP_CHUNK
    for g in range(N_GROUPS):
        cu = slice(g * GROUP_W, (g + 1) * GROUP_W)
        cv = slice(A_W + g * GROUP_W, A_W + (g + 1) * GROUP_W)
        vn = _rms(_gelu_tanh(z[:, cv]), gv_ref[:, cu]).astype(BF16)
        vn_wide = jnp.concatenate(
            [vn[cc * GMLP_CHUNK:(cc + 1) * GMLP_CHUNK, :] for cc in range(n_g)], axis=1)
        sv_wide = _dot(ws_ref[g], vn_wide)
        for cc in range(n_g):
            rows = slice(cc * GMLP_CHUNK, (cc + 1) * GMLP_CHUNK)
            sv = sv_wide[:, cc * GROUP_W:(cc + 1) * GROUP_W] + bs_ref[g]
            a_ref[rows, cu] = (_gelu_tanh(z[rows, cu]) * sv).astype(BF16)

    c0 = 2 * A_W
    q_all = _silu(z[:, c0:c0 + B_W])
    kf_all, gf_all = _forget(z[:, c0 + B_W:c0 + 2 * B_W], lbf_ref[...])
    kb_all, gb_all = _forget(z[:, c0 + 2 * B_W:c0 + 3 * B_W], lbb_ref[...])
    v_all = z[:, c0 + 3 * B_W:c0 + 4 * B_W].astype(BF16)
    og_ref[...] = _silu(z[:, c0 + 4 * B_W:c0 + 5 * B_W]).astype(BF16)

    ti = lax.broadcasted_iota(jnp.int32, (C, C), 0)
    si = lax.broadcasted_iota(jnp.int32, (C, C), 1)
    tri = (si <= ti).astype(BF16)
    tri3 = jnp.concatenate([tri, tri, tri], axis=1)
    shift_blk = GLA_BLOCK.bit_length() - 1
    same_blk = lax.shift_right_logical(ti, shift_blk) == lax.shift_right_logical(si, shift_blk)
    diag_f = same_blk & (si <= ti)
    diag_b = same_blk & (si >= ti)
    parent_masks = [
        None if 2 * m == C else (lax.shift_right_logical(ti, (2 * m).bit_length() - 1)
                                 == lax.shift_right_logical(si, (2 * m).bit_length() - 1))
        for m in _gla_levels()]

    g_cat = jnp.concatenate([gf_all, gb_all], axis=1)
    prefix = [_cumsum_rows(g_cat[c * C:(c + 1) * C, :], tri3) for c in range(n_c)]

    def prep(c):
        rs = slice(c * C, (c + 1) * C)
        q = q_all[rs, :]
        kf = kf_all[rs, :]
        kb = kb_all[rs, :]
        gb = gb_all[rs, :]
        cs = prefix[c]
        bf = cs[:, :B_W]
        cb = cs[:, B_W:]
        eb = cb - gb
        diag_f_ops, lev_f = _score_operands(q, kf, bf, forward=True)
        diag_b_ops, lev_b = _score_operands(q, kb, eb, forward=False)
        bf_end = bf[C - 1:C, :]
        gb_tot = cb[C - 1:C, :]
        qb_ref[rs, :] = (q * jnp.exp(gb_tot - eb)).astype(BF16)
        dg_ref[c:c + 1, :] = jnp.exp(gb_tot)
        return dict(
            c=c, rs=rs, v=v_all[rs, :], diag_f=diag_f_ops, diag_b=diag_b_ops, lev_f=lev_f, lev_b=lev_b,
            q_in=(q * jnp.exp(bf)).astype(BF16),
            k_out_f=(kf * jnp.exp(bf_end - bf)).astype(BF16),
            k_out_b=(kb * jnp.exp(eb)).astype(BF16),
            dec_f=jnp.exp(bf_end))

    def raw_scores(p):
        out = []
        for hd in range(N_HEADS):
            hs = slice(hd * GROUP_W, (hd + 1) * GROUP_W)
            lv = [_dot_nt(*_level_operands(lf[:, hs], lb[:, hs], m))
                  for m, lf, lb in zip(_gla_levels(), p["lev_f"], p["lev_b"])]
            out.append((lv,
                        _dot_nt(p["diag_f"][0][:, hs], p["diag_f"][1][:, hs]),
                        _dot_nt(p["diag_b"][0][:, hs], p["diag_b"][1][:, hs])))
        return out

    def finish(p, raw):
        rs, v = p["rs"], p["v"]
        for hd in range(N_HEADS):
            hs = slice(hd * GROUP_W, (hd + 1) * GROUP_W)
            lv, df, db = raw[hd]
            scores = jnp.where(diag_f, df, 0.0) + jnp.where(diag_b, db, 0.0)
            for a, mk in zip(lv, parent_masks):
                scores = scores + (a if mk is None else jnp.where(mk, a, 0.0))
            st = state[hd]
            op_ref[rs, hs] = (_dot(scores.astype(BF16), v[:, hs])
                              + _dot_nt(p["q_in"][:, hs], st.astype(BF16))).astype(BF16)
            inc = _dot_tn(v[:, hs], jnp.concatenate([p["k_out_f"][:, hs], p["k_out_b"][:, hs]], axis=1))
            state[hd] = p["dec_f"][:, hs] * st + inc[:, :GROUP_W]
            ds_ref[p["c"], hd] = inc[:, GROUP_W:].astype(BF16)

    ready = {c: prep(c) for c in range(min(PREP_LEAD, n_c))}
    raw = {0: raw_scores(ready[0])}
    for c in range(n_c):
        if c + 1 < n_c:
            raw[c + 1] = raw_scores(ready[c + 1])
        if c + PREP_LEAD < n_c:
            ready[c + PREP_LEAD] = prep(c + PREP_LEAD)
        finish(ready.pop(c), raw.pop(c))

    for hd in range(N_HEADS):
        st_ref[hd] = state[hd]

    @pl.when(t_idx == pl.num_programs(1) - 1)
    def _():
        for hd in range(N_HEADS):
            sf_ref[hd] = state[hd]


def _mix1(x, mods, mod_row, gn, w_in, ws, bs, gv, lbf, lbb, s0, *, layer, tm):
    B, L, _ = x.shape
    n_t = L // tm
    n_c = tm // GLA_CHUNK
    tok = lambda w: pl.BlockSpec((None, tm, w), lambda b, t: (b, t, 0))
    state = pl.BlockSpec((None, N_HEADS, GROUP_W, GROUP_W), lambda b, t: (b, 0, 0, 0))
    half = jax.ShapeDtypeStruct((B, L, B_W), BF16)
    return pl.pallas_call(
        functools.partial(_mix1_kernel, tm=tm),
        grid=(B, n_t),
        in_specs=[
            tok(D_MODEL),
            _mod_spec(layer, mod_row),
            _layer_spec(layer, (1, D_MODEL)),
            _layer_spec(layer, (D_MODEL, IN_W)),
            _layer_spec(layer, (N_GROUPS, GMLP_CHUNK, GMLP_CHUNK)),
            _layer_spec(layer, (N_GROUPS, GMLP_CHUNK, GROUP_W)),
            _layer_spec(layer, (1, A_W)),
            _layer_spec(layer, (1, B_W)),
            _layer_spec(layer, (1, B_W)),
            state,
        ],
        out_specs=[
            tok(A_W), tok(B_W), tok(B_W), tok(B_W),
            pl.BlockSpec((None, n_c, N_HEADS, GROUP_W, GROUP_W), lambda b, t: (b, t, 0, 0, 0)),
            pl.BlockSpec((None, None, n_c, B_W), lambda b, t: (b, t, 0, 0)),
            state,
        ],
        out_shape=[
            half, half, half, half,
            jax.ShapeDtypeStruct((B, L // GLA_CHUNK, N_HEADS, GROUP_W, GROUP_W), BF16),
            jax.ShapeDtypeStruct((B, n_t, n_c, B_W), F32),
            jax.ShapeDtypeStruct((B, N_HEADS, GROUP_W, GROUP_W), F32),
        ],
        scratch_shapes=[pltpu.VMEM((N_HEADS, GROUP_W, GROUP_W), F32)],
        compiler_params=pltpu.CompilerParams(
            dimension_semantics=("arbitrary", "arbitrary"), vmem_limit_bytes=VMEM_LIMIT),
        name="mixer_forward",
    )(x, mods, gn.reshape(DEPTH, 1, D_MODEL), w_in, ws, bs, gv.reshape(DEPTH, 1, A_W),
      lbf.reshape(DEPTH, 1, B_W), lbb.reshape(DEPTH, 1, B_W), s0)


def _mix2_kernel(x_ref, mod_ref, a_ref, og_ref, op_ref, qb_ref, ds_ref, dg_ref, go_ref,
                 wout_ref, s0_ref, o_ref, sb_ref, st_ref, *, tm):
    C = GLA_CHUNK
    n_c = tm // C
    t_idx = pl.program_id(1)

    @pl.when(t_idx == 0)
    def _():
        st_ref[...] = s0_ref[...]

    state = [st_ref[hd] for hd in range(N_HEADS)]
    o_rows = [None] * n_c
    for c in reversed(range(n_c)):
        rs = slice(c * C, (c + 1) * C)
        qb = qb_ref[rs, :]
        dg = dg_ref[c:c + 1, :]
        o_heads = []
        for hd in range(N_HEADS):
            hs = slice(hd * GROUP_W, (hd + 1) * GROUP_W)
            st = state[hd]
            o_heads.append(op_ref[rs, hs].astype(F32) + _dot_nt(qb[:, hs], st.astype(BF16)))
            state[hd] = dg[:, hs] * st + ds_ref[c, hd].astype(F32)
        o_rows[c] = o_heads

    for hd in range(N_HEADS):
        st_ref[hd] = state[hd]

    @pl.when(t_idx == pl.num_programs(1) - 1)
    def _():
        for hd in range(N_HEADS):
            sb_ref[hd] = state[hd]

    y = _dot(a_ref[...], wout_ref[0:A_W, :])
    for hd in range(N_HEADS):
        hs = slice(hd * GROUP_W, (hd + 1) * GROUP_W)
        o_h = jnp.concatenate([o_rows[c][hd] for c in range(n_c)], axis=0)
        on = _rms(o_h, go_ref[:, hs]) * og_ref[:, hs].astype(F32)
        y = y + _dot(on.astype(BF16), wout_ref[A_W + hd * GROUP_W:A_W + (hd + 1) * GROUP_W, :])
    o_ref[...] = x_ref[...] + mod_ref[5:6, :] * y


def _mix2(x, mods, mod_row, a, og, op, qb, ds, dg, go, w_out, s0, *, layer, tm):
    B, L, _ = x.shape
    n_t = L // tm
    n_c = tm // GLA_CHUNK
    tok = lambda w: pl.BlockSpec((None, tm, w), lambda b, t: (b, n_t - 1 - t, 0))
    state = pl.BlockSpec((None, N_HEADS, GROUP_W, GROUP_W), lambda b, t: (b, 0, 0, 0))
    return pl.pallas_call(
        functools.partial(_mix2_kernel, tm=tm),
        grid=(B, n_t),
        in_specs=[
            tok(D_MODEL),
            _mod_spec(layer, mod_row),
            tok(A_W), tok(B_W), tok(B_W), tok(B_W),
            pl.BlockSpec((None, n_c, N_HEADS, GROUP_W, GROUP_W), lambda b, t: (b, n_t - 1 - t, 0, 0, 0)),
            pl.BlockSpec((None, None, n_c, B_W), lambda b, t: (b, n_t - 1 - t, 0, 0)),
            _layer_spec(layer, (1, B_W)),
            _layer_spec(layer, (D_MODEL, D_MODEL)),
            state,
        ],
        out_specs=[tok(D_MODEL), state],
        out_shape=[
            jax.ShapeDtypeStruct(x.shape, F32),
            jax.ShapeDtypeStruct((B, N_HEADS, GROUP_W, GROUP_W), F32),
        ],
        scratch_shapes=[pltpu.VMEM((N_HEADS, GROUP_W, GROUP_W), F32)],
        compiler_params=pltpu.CompilerParams(
            dimension_semantics=("arbitrary", "arbitrary"), vmem_limit_bytes=VMEM_LIMIT),
        name="mixer_backward",
    )(x, mods, a, og, op, qb, ds, dg, go.reshape(DEPTH, 1, B_W), w_out, s0)


def _grid_pos_embed(rows):
    quarter = D_MODEL // 4
    freq = POS_THETA ** (-jnp.arange(quarter, dtype=F32) / quarter)

    def axis_embed(n):
        a = jnp.arange(n).astype(F32)[:, None] * freq[None, :]
        return jnp.concatenate([jnp.sin(a), jnp.cos(a)], axis=-1)

    row_e = jnp.repeat(axis_embed(rows), GRID_W, axis=0)
    col_e = jnp.tile(axis_embed(GRID_W), (rows, 1))
    return jnp.concatenate([row_e, col_e], axis=-1)


def _lower_bounds(p):
    cum = jnp.cumsum(jax.nn.softmax(p.astype(F32), axis=0), axis=0)
    return cum - cum[0:1]


def _tile(L, want):
    tm = min(L, want)
    assert L % tm == 0 and tm % GMLP_CHUNK == 0 and tm % GLA_CHUNK == 0
    return tm


def kernel(x, c, ctx, c_ctx, w_ada, b_ada, norm_ffn1_g, norm_mix_g, norm_ffn2_g, ffn1_w1, ffn1_w3,
           ffn1_w2, ffn2_w1, ffn2_w3, ffn2_w2, w_in, w_out, gmlp_ws, gmlp_bs, gmlp_norm_g,
           hgrn_lb_fwd, hgrn_lb_bwd, hgrn_norm_g, norm_final_g):
    B, L, _ = x.shape
    n_ctx = ctx.shape[1]
    assert B + 1 <= MOD_ROWS
    tm_lat = _tile(L, 512)
    tm_mix2 = _tile(L, 1024)
    tm_ffn = _tile(L, 1024)
    tm_ctx = _tile(n_ctx, 512)
    tm_ctx_ffn = _tile(B * n_ctx, 1024)

    cond = jnp.concatenate([c, c_ctx[None], jnp.zeros((MOD_ROWS - B - 1, D_MODEL), F32)], axis=0)
    mods = _modulation(cond, w_ada, b_ada).reshape(DEPTH, MOD_ROWS, N_MOD, D_MODEL)
    lat_row = lambda b: b
    ctx_row = lambda b: B

    pos = _grid_pos_embed(L // GRID_W)
    lbs_f = _lower_bounds(hgrn_lb_fwd)
    lbs_b = _lower_bounds(hgrn_lb_bwd)
    bs_full = jnp.broadcast_to(gmlp_bs[..., None], gmlp_bs.shape + (GROUP_W,))
    ffn1 = [w.astype(BF16) for w in (ffn1_w1, ffn1_w3, ffn1_w2)]
    ffn2 = [w.astype(BF16) for w in (ffn2_w1, ffn2_w3, ffn2_w2)]
    zero_state = jnp.zeros((B, N_HEADS, GROUP_W, GROUP_W), F32)
    mix1 = functools.partial(_mix1, gn=norm_mix_g, w_in=w_in.astype(BF16), ws=gmlp_ws.astype(BF16),
                             bs=bs_full, gv=gmlp_norm_g, lbf=lbs_f, lbb=lbs_b)
    mix2 = functools.partial(_mix2, go=hgrn_norm_g, w_out=w_out.astype(BF16))

    def ctx_ffn(cx, g, w1, w3, w2, **kw):
        flat = cx.reshape(1, B * n_ctx, D_MODEL)
        return _ffn(flat, mods, ctx_row, g, w1, w3, w2, tm=tm_ctx_ffn, **kw).reshape(B, n_ctx, D_MODEL)

    lat, cx = x, ctx
    for l in range(DEPTH):
        last = l == DEPTH - 1
        lat = _ffn(lat, mods, lat_row, norm_ffn1_g, *ffn1, layer=l, j0=0, tm=tm_ffn,
                   pos=pos if l == 0 else None)
        cx = ctx_ffn(cx, norm_ffn1_g, *ffn1, layer=l, j0=0)

        *parts_c, s_f = mix1(cx, mods, ctx_row, s0=zero_state, layer=l, tm=tm_ctx)
        cx_mixed, s_b = mix2(cx, mods, ctx_row, *parts_c, s0=zero_state, layer=l, tm=tm_ctx)
        *parts_l, dg_l, _ = mix1(lat, mods, lat_row, s0=s_f, layer=l, tm=tm_lat)
        dg_l = dg_l.reshape(B, L // tm_mix2, tm_mix2 // GLA_CHUNK, B_W)
        lat, _ = mix2(lat, mods, lat_row, *parts_l, dg_l, s0=s_b, layer=l, tm=tm_mix2)

        lat = _ffn(lat, mods, lat_row, norm_ffn2_g, *ffn2, layer=l, j0=6, tm=tm_ffn,
                   gfin=norm_final_g if last else None)
        if not last:
            cx = ctx_ffn(cx_mixed, norm_ffn2_g, *ffn2, layer=l, j0=6)
    return lat
```

```python
import functools

import jax
import jax.numpy as jnp
from jax import lax
from jax.experimental import pallas as pl
from jax.experimental.pallas import tpu as pltpu

D_MODEL = 1024
D_FF = 2816
DEPTH = 4
GRID_W = 64
A_W = 512
B_W = 512
IN_W = 2 * A_W + 5 * B_W
N_GROUPS = 4
GROUP_W = 128
GMLP_CHUNK = 128
N_HEADS = 4
N_MOD = 9
MOD_ROWS = 16
MOD_COLS = 3 * D_MODEL
EPS = 1e-6
POS_THETA = 10000.0

GLA_CHUNK = 64
GLA_BLOCK = 16
LOG_F_MIN = -10.0
SIDE_W = 256
FF_SPLITS = ((0, 768), (768, 768), (1536, 768), (2304, 512))

V7X_VMEM_BYTES = 64 * 1024 * 1024
VMEM_LIMIT = V7X_VMEM_BYTES - 8 * 1024 * 1024

F32 = jnp.float32
BF16 = jnp.bfloat16


def _silu(x):
    hx = 0.5 * x
    return hx + hx * jnp.tanh(hx)


def _gelu_tanh(x):
    hx = 0.5 * x
    return hx + hx * jnp.tanh(x * (0.7978845608028654 + 0.035677408136300125 * (x * x)))


def _rms(x, g):
    return x * lax.rsqrt(jnp.mean(x * x, axis=-1, keepdims=True) + EPS) * g


def _modulated_rms(x, g, shift, scale):
    return x * lax.rsqrt(jnp.mean(x * x, axis=-1, keepdims=True) + EPS) * (g * (1.0 + scale)) + shift


def _dot(a, b):
    return jnp.dot(a, b, preferred_element_type=F32)


def _dot_nt(a, b):
    return lax.dot_general(a, b, (((1,), (1,)), ((), ())), preferred_element_type=F32)


def _dot_tn(a, b):
    return lax.dot_general(a, b, (((0,), (0,)), ((), ())), preferred_element_type=F32)


def _layer_spec(layer, shape):
    nd = len(shape)
    return pl.BlockSpec((None,) + shape, lambda *_: (layer,) + (0,) * nd, pipeline_mode=pl.Buffered(1))


def _mod_spec(layer, mod_row):
    return pl.BlockSpec((None, None, N_MOD, D_MODEL), lambda b, t: (layer, mod_row(b), 0, 0))


def _mod_kernel(cond_ref, w_ref, b_ref, o_ref):
    s = _silu(cond_ref[...])
    o_ref[...] = jnp.dot(s, w_ref[...], preferred_element_type=F32,
                         precision=lax.Precision.HIGHEST) + b_ref[...]


def _modulation(cond, w_ada, b_ada):
    n_out = N_MOD * D_MODEL
    tn = MOD_COLS
    return pl.pallas_call(
        _mod_kernel,
        grid=(DEPTH, n_out // tn),
        in_specs=[
            pl.BlockSpec((MOD_ROWS, D_MODEL), lambda l, j: (0, 0)),
            pl.BlockSpec((None, D_MODEL, tn), lambda l, j: (l, 0, j)),
            pl.BlockSpec((None, 1, tn), lambda l, j: (l, 0, j)),
        ],
        out_specs=pl.BlockSpec((None, MOD_ROWS, tn), lambda l, j: (l, 0, j)),
        out_shape=jax.ShapeDtypeStruct((DEPTH, MOD_ROWS, n_out), F32),
        compiler_params=pltpu.CompilerParams(
            dimension_semantics=("arbitrary", "arbitrary"), vmem_limit_bytes=VMEM_LIMIT),
        name="adaln_modulation",
    )(cond, w_ada, b_ada.reshape(DEPTH, 1, n_out))


def _ffn_kernel(*refs, j0, add_pos, final):
    refs = list(refs)
    x_ref = refs.pop(0)
    pos_ref = refs.pop(0) if add_pos else None
    mod_ref, g_ref, w1_ref, w3_ref, w2_ref = refs[:5]
    refs = refs[5:]
    gfin_ref = refs.pop(0) if final else None
    o_ref = refs.pop(0)

    x = x_ref[...]
    if add_pos:
        x = x + pos_ref[...]
    shift = mod_ref[j0:j0 + 1, :]
    scale = mod_ref[j0 + 1:j0 + 2, :]
    gate = mod_ref[j0 + 2:j0 + 3, :]
    h = _modulated_rms(x, g_ref[...], shift, scale).astype(BF16)
    y = None
    for s, n in FF_SPLITS:
        a = _dot(h, w1_ref[:, s:s + n])
        b = _dot(h, w3_ref[:, s:s + n])
        p = (_silu(a) * b).astype(BF16)
        yc = _dot(p, w2_ref[s:s + n, :])
        y = yc if y is None else y + yc
    out = x + (0.5 * gate) * y
    if final:
        out = _rms(out, gfin_ref[...])
    o_ref[...] = out


def _ffn(x, mods, mod_row, g, w1, w3, w2, *, layer, j0, tm, pos=None, gfin=None):
    B, L, _ = x.shape
    add_pos, final = pos is not None, gfin is not None
    tok = pl.BlockSpec((None, tm, D_MODEL), lambda b, t: (b, t, 0))
    in_specs, args = [tok], [x]
    if add_pos:
        in_specs.append(pl.BlockSpec((tm, D_MODEL), lambda b, t: (t, 0)))
        args.append(pos)
    in_specs += [
        _mod_spec(layer, mod_row),
        _layer_spec(layer, (1, D_MODEL)),
        _layer_spec(layer, (D_MODEL, D_FF)),
        _layer_spec(layer, (D_MODEL, D_FF)),
        _layer_spec(layer, (D_FF, D_MODEL)),
    ]
    args += [mods, g.reshape(DEPTH, 1, D_MODEL), w1, w3, w2]
    if final:
        in_specs.append(pl.BlockSpec((1, D_MODEL), lambda b, t: (0, 0)))
        args.append(gfin.reshape(1, D_MODEL))
    return pl.pallas_call(
        functools.partial(_ffn_kernel, j0=j0, add_pos=add_pos, final=final),
        grid=(B, L // tm),
        in_specs=in_specs,
        out_specs=tok,
        out_shape=jax.ShapeDtypeStruct(x.shape, F32),
        compiler_params=pltpu.CompilerParams(
            dimension_semantics=("parallel", "parallel"), vmem_limit_bytes=VMEM_LIMIT),
        name="adaln_swiglu",
    )(*args)


def _forget(f_raw, lb):
    t = jnp.tanh(0.5 * f_raw)
    half = 0.5 * (1.0 - lb)
    k = half - half * t
    log_f = jnp.maximum(jnp.log((lb + half) + half * t), LOG_F_MIN)
    return k, log_f


def _cumsum_rows(g, tri2):
    hi = g.astype(BF16)
    lo = (g - hi.astype(F32)).astype(BF16)
    return _dot(tri2, jnp.concatenate([hi, lo], axis=0))


def _ref_rows(a, first, period):
    n = a.shape[0] // period
    parts = [jnp.broadcast_to(a[first + j * period:first + j * period + 1, :], (period, a.shape[1]))
             for j in range(n)]
    return parts[0] if n == 1 else jnp.concatenate(parts, axis=0)


def _gla_levels():
    levels = []
    m = GLA_BLOCK
    while m < GLA_CHUNK:
        levels.append(m)
        m *= 2
    return tuple(levels)


def _interleave_rows(first, second, m):
    C = first.shape[0]
    parts = [(first if (i // m) % 2 == 0 else second)[i:i + m] for i in range(0, C, m)]
    return jnp.concatenate(parts, axis=0)


def _score_operands(q, k, b, *, forward):
    half = GLA_BLOCK // 2
    d = b - _ref_rows(b, half - 1 if forward else half, GLA_BLOCK)
    if not forward:
        d = -d
    diag = ((q * jnp.exp(d)).astype(BF16), (k * jnp.exp(-d)).astype(BF16))
    levels = []
    for m in _gla_levels():
        d = b - _ref_rows(b, m - 1 if forward else m, 2 * m)
        if forward:
            qk, expo = _interleave_rows(k, q, m), _interleave_rows(-d, d, m)
        else:
            qk, expo = _interleave_rows(q, k, m), _interleave_rows(-d, d, m)
        levels.append((qk * jnp.exp(expo)).astype(BF16))
    return diag, levels


def _level_operands(lf, lb, m):
    zf, zb = jnp.zeros_like(lf), jnp.zeros_like(lb)
    q_side = jnp.concatenate([_interleave_rows(zf, lf, m), _interleave_rows(lb, zb, m)], axis=1)
    k_side = jnp.concatenate([_interleave_rows(lf, zf, m), _interleave_rows(zb, lb, m)], axis=1)
    return q_side, k_side


def _mix1_kernel(x_ref, mod_ref, gn_ref, win_ref, ws_ref, bs_ref, gv_ref, lbf_ref, lbb_ref, s0_ref,
                 a_ref, og_ref, op_ref, qb_ref, ds_ref, dg_ref, sf_ref, st_ref, *, tm):
    C = GLA_CHUNK
    n_c = tm // C
    t_idx = pl.program_id(1)

    @pl.when(t_idx == 0)
    def _():
        st_ref[...] = s0_ref[...]

    state = [st_ref[hd] for hd in range(N_HEADS)]

    x = x_ref[...]
    shift = mod_ref[3:4, :]
    scale = mod_ref[4:5, :]
    h = _modulated_rms(x, gn_ref[...], shift, scale).astype(BF16)
    def proj(lo, width):
        return _dot(h, win_ref[:, lo:lo + width])

    c0 = 2 * A_W
    q_all = _silu(proj(c0, B_W))
    kf_all, gf_all = _forget(proj(c0 + B_W, B_W), lbf_ref[...])
    kb_all, gb_all = _forget(proj(c0 + 2 * B_W, B_W), lbb_ref[...])
    v_all = proj(c0 + 3 * B_W, B_W).astype(BF16)

    ti = lax.broadcasted_iota(jnp.int32, (C, C), 0)
    si = lax.broadcasted_iota(jnp.int32, (C, C), 1)
    tri = (si <= ti).astype(BF16)
    tri2 = jnp.concatenate([tri, tri], axis=1)
    shift_blk = GLA_BLOCK.bit_length() - 1
    same_blk = lax.shift_right_logical(ti, shift_blk) == lax.shift_right_logical(si, shift_blk)
    diag_f = same_blk & (si <= ti)
    diag_b = same_blk & (si >= ti)
    parent_masks = [
        None if 2 * m == C else (lax.shift_right_logical(ti, (2 * m).bit_length() - 1)
                                 == lax.shift_right_logical(si, (2 * m).bit_length() - 1))
        for m in _gla_levels()]

    g_cat = jnp.concatenate([gf_all, gb_all], axis=1)
    prefix = [_cumsum_rows(g_cat[c * C:(c + 1) * C, :], tri2) for c in range(n_c)]

    def prep(c):
        rs = slice(c * C, (c + 1) * C)
        q = q_all[rs, :]
        kf = kf_all[rs, :]
        kb = kb_all[rs, :]
        gb = gb_all[rs, :]
        cs = prefix[c]
        bf = cs[:, :B_W]
        cb = cs[:, B_W:]
        eb = cb - gb
        diag_f_ops, lev_f = _score_operands(q, kf, bf, forward=True)
        diag_b_ops, lev_b = _score_operands(q, kb, eb, forward=False)
        bf_end = bf[C - 1:C, :]
        gb_tot = cb[C - 1:C, :]
        qb_ref[rs, :] = (q * jnp.exp(gb_tot - eb)).astype(BF16)
        dg_ref[c:c + 1, :] = jnp.exp(gb_tot)
        return dict(
            c=c, rs=rs, v=v_all[rs, :], diag_f=diag_f_ops, diag_b=diag_b_ops, lev_f=lev_f, lev_b=lev_b,
            q_in=(q * jnp.exp(bf)).astype(BF16),
            k_out_f=(kf * jnp.exp(bf_end - bf)).astype(BF16),
            k_out_b=(kb * jnp.exp(eb)).astype(BF16),
            dec_f=jnp.exp(bf_end))

    def raw_scores(p):
        out = []
        for hd in range(N_HEADS):
            hs = slice(hd * GROUP_W, (hd + 1) * GROUP_W)
            lv = [_dot_nt(*_level_operands(lf[:, hs], lb[:, hs], m))
                  for m, lf, lb in zip(_gla_levels(), p["lev_f"], p["lev_b"])]
            out.append((lv,
                        _dot_nt(p["diag_f"][0][:, hs], p["diag_f"][1][:, hs]),
                        _dot_nt(p["diag_b"][0][:, hs], p["diag_b"][1][:, hs])))
        return out

    def finish(p, raw):
        rs, v = p["rs"], p["v"]
        for hd in range(N_HEADS):
            hs = slice(hd * GROUP_W, (hd + 1) * GROUP_W)
            lv, df, db = raw[hd]
            scores = jnp.where(diag_f, df, 0.0) + jnp.where(diag_b, db, 0.0)
            for a, mk in zip(lv, parent_masks):
                scores = scores + (a if mk is None else jnp.where(mk, a, 0.0))
            st = state[hd]
            op_ref[rs, hs] = (_dot(scores.astype(BF16), v[:, hs])
                              + _dot_nt(p["q_in"][:, hs], st.astype(BF16))).astype(BF16)
            inc = _dot_tn(v[:, hs], jnp.concatenate([p["k_out_f"][:, hs], p["k_out_b"][:, hs]], axis=1))
            state[hd] = p["dec_f"][:, hs] * st + inc[:, :GROUP_W]
            ds_ref[p["c"], hd] = inc[:, GROUP_W:].astype(BF16)

    side_cols = [c0 + 4 * B_W, A_W, 0]
    side_jobs = [(lo + j * SIDE_W, SIDE_W) for lo in side_cols for j in range(B_W // SIDE_W)]
    side = []
    for c in range(max(n_c, len(side_jobs))):
        if c < n_c:
            p = prep(c)
            finish(p, raw_scores(p))
        if c < len(side_jobs):
            side.append(proj(*side_jobs[c]))
    per = B_W // SIDE_W
    og_ref[...] = _silu(jnp.concatenate(side[0:per], axis=1)).astype(BF16)
    z_v = jnp.concatenate(side[per:2 * per], axis=1)
    z_u = jnp.concatenate(side[2 * per:3 * per], axis=1)

    n_g = tm // GMLP_CHUNK
    for g in range(N_GROUPS):
        cu = slice(g * GROUP_W, (g + 1) * GROUP_W)
        vn = _rms(_gelu_tanh(z_v[:, cu]), gv_ref[:, cu]).astype(BF16)
        vn_wide = jnp.concatenate(
            [vn[cc * GMLP_CHUNK:(cc + 1) * GMLP_CHUNK, :] for cc in range(n_g)], axis=1)
        sv_wide = _dot(ws_ref[g], vn_wide)
        for cc in range(n_g):
            rows = slice(cc * GMLP_CHUNK, (cc + 1) * GMLP_CHUNK)
            sv = sv_wide[:, cc * GROUP_W:(cc + 1) * GROUP_W] + bs_ref[g]
            a_ref[rows, cu] = (_gelu_tanh(z_u[rows, cu]) * sv).astype(BF16)

    for hd in range(N_HEADS):
        st_ref[hd] = state[hd]

    @pl.when(t_idx == pl.num_programs(1) - 1)
    def _():
        for hd in range(N_HEADS):
            sf_ref[hd] = state[hd]


def _mix1(x, mods, mod_row, gn, w_in, ws, bs, gv, lbf, lbb, s0, *, layer, tm):
    B, L, _ = x.shape
    n_t = L // tm
    n_c = tm // GLA_CHUNK
    tok = lambda w: pl.BlockSpec((None, tm, w), lambda b, t: (b, t, 0))
    state = pl.BlockSpec((None, N_HEADS, GROUP_W, GROUP_W), lambda b, t: (b, 0, 0, 0))
    half = jax.ShapeDtypeStruct((B, L, B_W), BF16)
    return pl.pallas_call(
        functools.partial(_mix1_kernel, tm=tm),
        grid=(B, n_t),
        in_specs=[
            tok(D_MODEL),
            _mod_spec(layer, mod_row),
            _layer_spec(layer, (1, D_MODEL)),
            _layer_spec(layer, (D_MODEL, IN_W)),
            _layer_spec(layer, (N_GROUPS, GMLP_CHUNK, GMLP_CHUNK)),
            _layer_spec(layer, (N_GROUPS, GMLP_CHUNK, GROUP_W)),
            _layer_spec(layer, (1, A_W)),
            _layer_spec(layer, (1, B_W)),
            _layer_spec(layer, (1, B_W)),
            state,
        ],
        out_specs=[
            tok(A_W), tok(B_W), tok(B_W), tok(B_W),
            pl.BlockSpec((None, n_c, N_HEADS, GROUP_W, GROUP_W), lambda b, t: (b, t, 0, 0, 0)),
            pl.BlockSpec((None, None, n_c, B_W), lambda b, t: (b, t, 0, 0)),
            state,
        ],
        out_shape=[
            half, half, half, half,
            jax.ShapeDtypeStruct((B, L // GLA_CHUNK, N_HEADS, GROUP_W, GROUP_W), BF16),
            jax.ShapeDtypeStruct((B, n_t, n_c, B_W), F32),
            jax.ShapeDtypeStruct((B, N_HEADS, GROUP_W, GROUP_W), F32),
        ],
        scratch_shapes=[pltpu.VMEM((N_HEADS, GROUP_W, GROUP_W), F32)],
        compiler_params=pltpu.CompilerParams(
            dimension_semantics=("arbitrary", "arbitrary"), vmem_limit_bytes=VMEM_LIMIT),
        name="mixer_forward",
    )(x, mods, gn.reshape(DEPTH, 1, D_MODEL), w_in, ws, bs, gv.reshape(DEPTH, 1, A_W),
      lbf.reshape(DEPTH, 1, B_W), lbb.reshape(DEPTH, 1, B_W), s0)


def _mix2_kernel(x_ref, mod_ref, a_ref, og_ref, op_ref, qb_ref, ds_ref, dg_ref, go_ref,
                 wout_ref, s0_ref, o_ref, sb_ref, st_ref, *, tm):
    C = GLA_CHUNK
    n_c = tm // C
    t_idx = pl.program_id(1)

    @pl.when(t_idx == 0)
    def _():
        st_ref[...] = s0_ref[...]

    state = [st_ref[hd] for hd in range(N_HEADS)]
    o_rows = [None] * n_c
    for c in reversed(range(n_c)):
        rs = slice(c * C, (c + 1) * C)
        qb = qb_ref[rs, :]
        dg = dg_ref[c:c + 1, :]
        o_heads = []
        for hd in range(N_HEADS):
            hs = slice(hd * GROUP_W, (hd + 1) * GROUP_W)
            st = state[hd]
            o_heads.append(op_ref[rs, hs].astype(F32) + _dot_nt(qb[:, hs], st.astype(BF16)))
            state[hd] = dg[:, hs] * st + ds_ref[c, hd].astype(F32)
        o_rows[c] = o_heads

    for hd in range(N_HEADS):
        st_ref[hd] = state[hd]

    @pl.when(t_idx == pl.num_programs(1) - 1)
    def _():
        for hd in range(N_HEADS):
            sb_ref[hd] = state[hd]

    y = _dot(a_ref[...], wout_ref[0:A_W, :])
    for hd in range(N_HEADS):
        hs = slice(hd * GROUP_W, (hd + 1) * GROUP_W)
        o_h = jnp.concatenate([o_rows[c][hd] for c in range(n_c)], axis=0)
        on = _rms(o_h, go_ref[:, hs]) * og_ref[:, hs].astype(F32)
        y = y + _dot(on.astype(BF16), wout_ref[A_W + hd * GROUP_W:A_W + (hd + 1) * GROUP_W, :])
    o_ref[...] = x_ref[...] + mod_ref[5:6, :] * y


def _mix2(x, mods, mod_row, a, og, op, qb, ds, dg, go, w_out, s0, *, layer, tm):
    B, L, _ = x.shape
    n_t = L // tm
    n_c = tm // GLA_CHUNK
    tok = lambda w: pl.BlockSpec((None, tm, w), lambda b, t: (b, n_t - 1 - t, 0))
    state = pl.BlockSpec((None, N_HEADS, GROUP_W, GROUP_W), lambda b, t: (b, 0, 0, 0))
    return pl.pallas_call(
        functools.partial(_mix2_kernel, tm=tm),
        grid=(B, n_t),
        in_specs=[
            tok(D_MODEL),
            _mod_spec(layer, mod_row),
            tok(A_W), tok(B_W), tok(B_W), tok(B_W),
            pl.BlockSpec((None, n_c, N_HEADS, GROUP_W, GROUP_W), lambda b, t: (b, n_t - 1 - t, 0, 0, 0)),
            pl.BlockSpec((None, None, n_c, B_W), lambda b, t: (b, n_t - 1 - t, 0, 0)),
            _layer_spec(layer, (1, B_W)),
            _layer_spec(layer, (D_MODEL, D_MODEL)),
            state,
        ],
        out_specs=[tok(D_MODEL), state],
        out_shape=[
            jax.ShapeDtypeStruct(x.shape, F32),
            jax.ShapeDtypeStruct((B, N_HEADS, GROUP_W, GROUP_W), F32),
        ],
        scratch_shapes=[pltpu.VMEM((N_HEADS, GROUP_W, GROUP_W), F32)],
        compiler_params=pltpu.CompilerParams(
            dimension_semantics=("arbitrary", "arbitrary"), vmem_limit_bytes=VMEM_LIMIT),
        name="mixer_backward",
    )(x, mods, a, og, op, qb, ds, dg, go.reshape(DEPTH, 1, B_W), w_out, s0)


def _grid_pos_embed(rows):
    quarter = D_MODEL // 4
    freq = POS_THETA ** (-jnp.arange(quarter, dtype=F32) / quarter)

    def axis_embed(n):
        a = jnp.arange(n).astype(F32)[:, None] * freq[None, :]
        return jnp.concatenate([jnp.sin(a), jnp.cos(a)], axis=-1)

    row_e = jnp.repeat(axis_embed(rows), GRID_W, axis=0)
    col_e = jnp.tile(axis_embed(GRID_W), (rows, 1))
    return jnp.concatenate([row_e, col_e], axis=-1)


def _lower_bounds(p):
    cum = jnp.cumsum(jax.nn.softmax(p.astype(F32), axis=0), axis=0)
    return cum - cum[0:1]


def _tile(L, want):
    tm = min(L, want)
    assert L % tm == 0 and tm % GMLP_CHUNK == 0 and tm % GLA_CHUNK == 0
    return tm


def kernel(x, c, ctx, c_ctx, w_ada, b_ada, norm_ffn1_g, norm_mix_g, norm_ffn2_g, ffn1_w1, ffn1_w3,
           ffn1_w2, ffn2_w1, ffn2_w3, ffn2_w2, w_in, w_out, gmlp_ws, gmlp_bs, gmlp_norm_g,
           hgrn_lb_fwd, hgrn_lb_bwd, hgrn_norm_g, norm_final_g):
    B, L, _ = x.shape
    n_ctx = ctx.shape[1]
    assert B + 1 <= MOD_ROWS
    tm_lat = _tile(L, 512)
    tm_mix2 = _tile(L, 1024)
    tm_ffn = _tile(L, 1024)
    tm_ctx = _tile(n_ctx, 512)
    tm_ctx_ffn = _tile(B * n_ctx, 1024)

    cond = jnp.concatenate([c, c_ctx[None], jnp.zeros((MOD_ROWS - B - 1, D_MODEL), F32)], axis=0)
    mods = _modulation(cond, w_ada, b_ada).reshape(DEPTH, MOD_ROWS, N_MOD, D_MODEL)
    lat_row = lambda b: b
    ctx_row = lambda b: B

    pos = _grid_pos_embed(L // GRID_W)
    lbs_f = _lower_bounds(hgrn_lb_fwd)
    lbs_b = _lower_bounds(hgrn_lb_bwd)
    bs_full = jnp.broadcast_to(gmlp_bs[..., None], gmlp_bs.shape + (GROUP_W,))
    ffn1 = [w.astype(BF16) for w in (ffn1_w1, ffn1_w3, ffn1_w2)]
    ffn2 = [w.astype(BF16) for w in (ffn2_w1, ffn2_w3, ffn2_w2)]
    zero_state = jnp.zeros((B, N_HEADS, GROUP_W, GROUP_W), F32)
    mix1 = functools.partial(_mix1, gn=norm_mix_g, w_in=w_in.astype(BF16), ws=gmlp_ws.astype(BF16),
                             bs=bs_full, gv=gmlp_norm_g, lbf=lbs_f, lbb=lbs_b)
    mix2 = functools.partial(_mix2, go=hgrn_norm_g, w_out=w_out.astype(BF16))

    def ctx_ffn(cx, g, w1, w3, w2, **kw):
        flat = cx.reshape(1, B * n_ctx, D_MODEL)
        return _ffn(flat, mods, ctx_row, g, w1, w3, w2, tm=tm_ctx_ffn, **kw).reshape(B, n_ctx, D_MODEL)

    lat, cx = x, ctx
    for l in range(DEPTH):
        last = l == DEPTH - 1
        lat = _ffn(lat, mods, lat_row, norm_ffn1_g, *ffn1, layer=l, j0=0, tm=tm_ffn,
                   pos=pos if l == 0 else None)
        cx = ctx_ffn(cx, norm_ffn1_g, *ffn1, layer=l, j0=0)

        *parts_c, s_f = mix1(cx, mods, ctx_row, s0=zero_state, layer=l, tm=tm_ctx)
        cx_mixed, s_b = mix2(cx, mods, ctx_row, *parts_c, s0=zero_state, layer=l, tm=tm_ctx)
        *parts_l, dg_l, _ = mix1(lat, mods, lat_row, s0=s_f, layer=l, tm=tm_lat)
        dg_l = dg_l.reshape(B, L // tm_mix2, tm_mix2 // GLA_CHUNK, B_W)
        lat, _ = mix2(lat, mods, lat_row, *parts_l, dg_l, s0=s_b, layer=l, tm=tm_mix2)

        lat = _ffn(lat, mods, lat_row, norm_ffn2_g, *ffn2, layer=l, j0=6, tm=tm_ffn,
                   gfin=norm_final_g if last else None)
        if not last:
            cx = ctx_ffn(cx_mixed, norm_ffn2_g, *ffn2, layer=l, j0=6)
    return lat
```

```python
import functools

import jax
import jax.numpy as jnp
from jax import lax
from jax.experimental import pallas as pl
from jax.experimental.pallas import tpu as pltpu

D_MODEL = 1024
D_FF = 2816
DEPTH = 4
GRID_W = 64
A_W = 512
B_W = 512
IN_W = 2 * A_W + 5 * B_W
N_GROUPS = 4
GROUP_W = 128
GMLP_CHUNK = 128
N_HEADS = 4
N_MOD = 9
MOD_ROWS = 16
MOD_COLS = 3 * D_MODEL
EPS = 1e-6
POS_THETA = 10000.0

GLA_CHUNK = 64
GLA_BLOCK = 16
LOG_F_MIN = -10.0
SIDE_W = 256
FF_SPLITS = ((0, 768), (768, 768), (1536, 768), (2304, 512))

V7X_VMEM_BYTES = 64 * 1024 * 1024
VMEM_LIMIT = V7X_VMEM_BYTES - 8 * 1024 * 1024

F32 = jnp.float32
BF16 = jnp.bfloat16


def _silu(x):
    hx = 0.5 * x
    return hx + hx * jnp.tanh(hx)


def _gelu_tanh(x):
    hx = 0.5 * x
    return hx + hx * jnp.tanh(x * (0.7978845608028654 + 0.035677408136300125 * (x * x)))


def _rms(x, g):
    return x * lax.rsqrt(jnp.mean(x * x, axis=-1, keepdims=True) + EPS) * g


def _modulated_rms(x, g, shift, scale):
    return x * lax.rsqrt(jnp.mean(x * x, axis=-1, keepdims=True) + EPS) * (g * (1.0 + scale)) + shift


def _dot(a, b):
    return jnp.dot(a, b, preferred_element_type=F32)


def _dot_nt(a, b):
    return lax.dot_general(a, b, (((1,), (1,)), ((), ())), preferred_element_type=F32)


def _dot_tn(a, b):
    return lax.dot_general(a, b, (((0,), (0,)), ((), ())), preferred_element_type=F32)


def _layer_spec(layer, shape):
    nd = len(shape)
    return pl.BlockSpec((None,) + shape, lambda *_: (layer,) + (0,) * nd, pipeline_mode=pl.Buffered(1))


def _mod_spec(layer, mod_row):
    return pl.BlockSpec((None, None, N_MOD, D_MODEL), lambda b, t: (layer, mod_row(b), 0, 0))


def _mod_kernel(cond_ref, w_ref, b_ref, o_ref):
    s = _silu(cond_ref[...])
    o_ref[...] = jnp.dot(s, w_ref[...], preferred_element_type=F32,
                         precision=lax.Precision.HIGHEST) + b_ref[...]


def _modulation(cond, w_ada, b_ada):
    n_out = N_MOD * D_MODEL
    tn = MOD_COLS
    return pl.pallas_call(
        _mod_kernel,
        grid=(DEPTH, n_out // tn),
        in_specs=[
            pl.BlockSpec((MOD_ROWS, D_MODEL), lambda l, j: (0, 0)),
            pl.BlockSpec((None, D_MODEL, tn), lambda l, j: (l, 0, j)),
            pl.BlockSpec((None, 1, tn), lambda l, j: (l, 0, j)),
        ],
        out_specs=pl.BlockSpec((None, MOD_ROWS, tn), lambda l, j: (l, 0, j)),
        out_shape=jax.ShapeDtypeStruct((DEPTH, MOD_ROWS, n_out), F32),
        compiler_params=pltpu.CompilerParams(
            dimension_semantics=("arbitrary", "arbitrary"), vmem_limit_bytes=VMEM_LIMIT),
        name="adaln_modulation",
    )(cond, w_ada, b_ada.reshape(DEPTH, 1, n_out))


def _ffn_kernel(*refs, j0, add_pos, add_delta, final):
    refs = list(refs)
    x_ref = refs.pop(0)
    delta_ref = refs.pop(0) if add_delta else None
    pos_ref = refs.pop(0) if add_pos else None
    mod_ref, g_ref, w1_ref, w3_ref, w2_ref = refs[:5]
    refs = refs[5:]
    gfin_ref = refs.pop(0) if final else None
    o_ref = refs.pop(0)

    x = x_ref[...]
    if add_delta:
        x = x + delta_ref[...]
    if add_pos:
        x = x + pos_ref[...]
    shift = mod_ref[j0:j0 + 1, :]
    scale = mod_ref[j0 + 1:j0 + 2, :]
    gate = mod_ref[j0 + 2:j0 + 3, :]
    h = _modulated_rms(x, g_ref[...], shift, scale).astype(BF16)
    y = None
    for s, n in FF_SPLITS:
        a = _dot(h, w1_ref[:, s:s + n])
        b = _dot(h, w3_ref[:, s:s + n])
        p = (_silu(a) * b).astype(BF16)
        yc = _dot(p, w2_ref[s:s + n, :])
        y = yc if y is None else y + yc
    out = x + (0.5 * gate) * y
    if final:
        out = _rms(out, gfin_ref[...])
    o_ref[...] = out


def _ffn(x, mods, mod_row, g, w1, w3, w2, *, layer, j0, tm, pos=None, delta=None, gfin=None):
    B, L, _ = x.shape
    add_pos, add_delta, final = pos is not None, delta is not None, gfin is not None
    tok = pl.BlockSpec((None, tm, D_MODEL), lambda b, t: (b, t, 0))
    in_specs, args = [tok], [x]
    if add_delta:
        in_specs.append(tok)
        args.append(delta)
    if add_pos:
        in_specs.append(pl.BlockSpec((tm, D_MODEL), lambda b, t: (t, 0)))
        args.append(pos)
    in_specs += [
        _mod_spec(layer, mod_row),
        _layer_spec(layer, (1, D_MODEL)),
        _layer_spec(layer, (D_MODEL, D_FF)),
        _layer_spec(layer, (D_MODEL, D_FF)),
        _layer_spec(layer, (D_FF, D_MODEL)),
    ]
    args += [mods, g.reshape(DEPTH, 1, D_MODEL), w1, w3, w2]
    if final:
        in_specs.append(pl.BlockSpec((1, D_MODEL), lambda b, t: (0, 0)))
        args.append(gfin.reshape(1, D_MODEL))
    return pl.pallas_call(
        functools.partial(_ffn_kernel, j0=j0, add_pos=add_pos, add_delta=add_delta, final=final),
        grid=(B, L // tm),
        in_specs=in_specs,
        out_specs=tok,
        out_shape=jax.ShapeDtypeStruct(x.shape, F32),
        compiler_params=pltpu.CompilerParams(
            dimension_semantics=("parallel", "parallel"), vmem_limit_bytes=VMEM_LIMIT),
        name="adaln_swiglu",
    )(*args)


def _forget(f_raw, lb):
    t = jnp.tanh(0.5 * f_raw)
    half = 0.5 * (1.0 - lb)
    k = half - half * t
    log_f = jnp.maximum(jnp.log((lb + half) + half * t), LOG_F_MIN)
    return k, log_f


def _cumsum_rows(g, tri2):
    hi = g.astype(BF16)
    lo = (g - hi.astype(F32)).astype(BF16)
    return _dot(tri2, jnp.concatenate([hi, lo], axis=0))


def _ref_rows(a, first, period):
    n = a.shape[0] // period
    parts = [jnp.broadcast_to(a[first + j * period:first + j * period + 1, :], (period, a.shape[1]))
             for j in range(n)]
    return parts[0] if n == 1 else jnp.concatenate(parts, axis=0)


def _gla_levels():
    levels = []
    m = GLA_BLOCK
    while m < GLA_CHUNK:
        levels.append(m)
        m *= 2
    return tuple(levels)


def _interleave_rows(first, second, m):
    C = first.shape[0]
    parts = [(first if (i // m) % 2 == 0 else second)[i:i + m] for i in range(0, C, m)]
    return jnp.concatenate(parts, axis=0)


def _score_operands(q, k, b, *, forward):
    half = GLA_BLOCK // 2
    d = b - _ref_rows(b, half - 1 if forward else half, GLA_BLOCK)
    if not forward:
        d = -d
    diag = ((q * jnp.exp(d)).astype(BF16), (k * jnp.exp(-d)).astype(BF16))
    levels = []
    for m in _gla_levels():
        d = b - _ref_rows(b, m - 1 if forward else m, 2 * m)
        if forward:
            qk, expo = _interleave_rows(k, q, m), _interleave_rows(-d, d, m)
        else:
            qk, expo = _interleave_rows(q, k, m), _interleave_rows(-d, d, m)
        levels.append((qk * jnp.exp(expo)).astype(BF16))
    return diag, levels


def _level_operands(lf, lb, m):
    zf, zb = jnp.zeros_like(lf), jnp.zeros_like(lb)
    q_side = jnp.concatenate([_interleave_rows(zf, lf, m), _interleave_rows(lb, zb, m)], axis=1)
    k_side = jnp.concatenate([_interleave_rows(lf, zf, m), _interleave_rows(zb, lb, m)], axis=1)
    return q_side, k_side


def _mix1_kernel(x_ref, mod_ref, gn_ref, win_ref, ws_ref, bs_ref, gv_ref, lbf_ref, lbb_ref, s0_ref,
                 a_ref, og_ref, op_ref, qb_ref, ds_ref, dg_ref, sf_ref, st_ref, *, tm):
    C = GLA_CHUNK
    n_c = tm // C
    t_idx = pl.program_id(1)

    @pl.when(t_idx == 0)
    def _():
        st_ref[...] = s0_ref[...]

    state = [st_ref[hd] for hd in range(N_HEADS)]

    x = x_ref[...]
    shift = mod_ref[3:4, :]
    scale = mod_ref[4:5, :]
    h = _modulated_rms(x, gn_ref[...], shift, scale).astype(BF16)
    def proj(lo, width):
        return _dot(h, win_ref[:, lo:lo + width])

    c0 = 2 * A_W
    q_all = _silu(proj(c0, B_W))
    kf_all, gf_all = _forget(proj(c0 + B_W, B_W), lbf_ref[...])
    kb_all, gb_all = _forget(proj(c0 + 2 * B_W, B_W), lbb_ref[...])
    v_all = proj(c0 + 3 * B_W, B_W).astype(BF16)

    ti = lax.broadcasted_iota(jnp.int32, (C, C), 0)
    si = lax.broadcasted_iota(jnp.int32, (C, C), 1)
    tri = (si <= ti).astype(BF16)
    tri2 = jnp.concatenate([tri, tri], axis=1)
    shift_blk = GLA_BLOCK.bit_length() - 1
    same_blk = lax.shift_right_logical(ti, shift_blk) == lax.shift_right_logical(si, shift_blk)
    diag_f = same_blk & (si <= ti)
    diag_b = same_blk & (si >= ti)
    parent_masks = [
        None if 2 * m == C else (lax.shift_right_logical(ti, (2 * m).bit_length() - 1)
                                 == lax.shift_right_logical(si, (2 * m).bit_length() - 1))
        for m in _gla_levels()]

    g_cat = jnp.concatenate([gf_all, gb_all], axis=1)
    prefix = [_cumsum_rows(g_cat[c * C:(c + 1) * C, :], tri2) for c in range(n_c)]

    def prep(c):
        rs = slice(c * C, (c + 1) * C)
        q = q_all[rs, :]
        kf = kf_all[rs, :]
        kb = kb_all[rs, :]
        gb = gb_all[rs, :]
        cs = prefix[c]
        bf = cs[:, :B_W]
        cb = cs[:, B_W:]
        eb = cb - gb
        diag_f_ops, lev_f = _score_operands(q, kf, bf, forward=True)
        diag_b_ops, lev_b = _score_operands(q, kb, eb, forward=False)
        bf_end = bf[C - 1:C, :]
        gb_tot = cb[C - 1:C, :]
        qb_ref[rs, :] = (q * jnp.exp(gb_tot - eb)).astype(BF16)
        dg_ref[c:c + 1, :] = jnp.exp(gb_tot)
        return dict(
            c=c, rs=rs, v=v_all[rs, :], diag_f=diag_f_ops, diag_b=diag_b_ops, lev_f=lev_f, lev_b=lev_b,
            q_in=(q * jnp.exp(bf)).astype(BF16),
            k_out_f=(kf * jnp.exp(bf_end - bf)).astype(BF16),
            k_out_b=(kb * jnp.exp(eb)).astype(BF16),
            dec_f=jnp.exp(bf_end))

    def raw_scores(p):
        out = []
        for hd in range(N_HEADS):
            hs = slice(hd * GROUP_W, (hd + 1) * GROUP_W)
            lv = [_dot_nt(*_level_operands(lf[:, hs], lb[:, hs], m))
                  for m, lf, lb in zip(_gla_levels(), p["lev_f"], p["lev_b"])]
            out.append((lv,
                        _dot_nt(p["diag_f"][0][:, hs], p["diag_f"][1][:, hs]),
                        _dot_nt(p["diag_b"][0][:, hs], p["diag_b"][1][:, hs])))
        return out

    def finish(p, raw):
        rs, v = p["rs"], p["v"]
        for hd in range(N_HEADS):
            hs = slice(hd * GROUP_W, (hd + 1) * GROUP_W)
            lv, df, db = raw[hd]
            scores = jnp.where(diag_f, df, 0.0) + jnp.where(diag_b, db, 0.0)
            for a, mk in zip(lv, parent_masks):
                scores = scores + (a if mk is None else jnp.where(mk, a, 0.0))
            st = state[hd]
            op_ref[rs, hs] = (_dot(scores.astype(BF16), v[:, hs])
                              + _dot_nt(p["q_in"][:, hs], st.astype(BF16))).astype(BF16)
            inc = _dot_tn(v[:, hs], jnp.concatenate([p["k_out_f"][:, hs], p["k_out_b"][:, hs]], axis=1))
            state[hd] = p["dec_f"][:, hs] * st + inc[:, :GROUP_W]
            ds_ref[p["c"], hd] = inc[:, GROUP_W:].astype(BF16)

    side_cols = [c0 + 4 * B_W, A_W, 0]
    side_jobs = [(lo + j * SIDE_W, SIDE_W) for lo in side_cols for j in range(B_W // SIDE_W)]
    side = []
    for c in range(max(n_c, len(side_jobs))):
        if c < n_c:
            p = prep(c)
            finish(p, raw_scores(p))
        if c < len(side_jobs):
            side.append(proj(*side_jobs[c]))
    per = B_W // SIDE_W
    og_ref[...] = _silu(jnp.concatenate(side[0:per], axis=1)).astype(BF16)
    z_v = jnp.concatenate(side[per:2 * per], axis=1)
    z_u = jnp.concatenate(side[2 * per:3 * per], axis=1)

    n_g = tm // GMLP_CHUNK
    for g in range(N_GROUPS):
        cu = slice(g * GROUP_W, (g + 1) * GROUP_W)
        vn = _rms(_gelu_tanh(z_v[:, cu]), gv_ref[:, cu]).astype(BF16)
        vn_wide = jnp.concatenate(
            [vn[cc * GMLP_CHUNK:(cc + 1) * GMLP_CHUNK, :] for cc in range(n_g)], axis=1)
        sv_wide = _dot(ws_ref[g], vn_wide)
        for cc in range(n_g):
            rows = slice(cc * GMLP_CHUNK, (cc + 1) * GMLP_CHUNK)
            sv = sv_wide[:, cc * GROUP_W:(cc + 1) * GROUP_W] + bs_ref[g]
            a_ref[rows, cu] = (_gelu_tanh(z_u[rows, cu]) * sv).astype(BF16)

    for hd in range(N_HEADS):
        st_ref[hd] = state[hd]

    @pl.when(t_idx == pl.num_programs(1) - 1)
    def _():
        for hd in range(N_HEADS):
            sf_ref[hd] = state[hd]


def _mix1(x, mods, mod_row, gn, w_in, ws, bs, gv, lbf, lbb, s0, *, layer, tm):
    B, L, _ = x.shape
    n_t = L // tm
    n_c = tm // GLA_CHUNK
    tok = lambda w: pl.BlockSpec((None, tm, w), lambda b, t: (b, t, 0))
    state = pl.BlockSpec((None, N_HEADS, GROUP_W, GROUP_W), lambda b, t: (b, 0, 0, 0))
    half = jax.ShapeDtypeStruct((B, L, B_W), BF16)
    return pl.pallas_call(
        functools.partial(_mix1_kernel, tm=tm),
        grid=(B, n_t),
        in_specs=[
            tok(D_MODEL),
            _mod_spec(layer, mod_row),
            _layer_spec(layer, (1, D_MODEL)),
            _layer_spec(layer, (D_MODEL, IN_W)),
            _layer_spec(layer, (N_GROUPS, GMLP_CHUNK, GMLP_CHUNK)),
            _layer_spec(layer, (N_GROUPS, GMLP_CHUNK, GROUP_W)),
            _layer_spec(layer, (1, A_W)),
            _layer_spec(layer, (1, B_W)),
            _layer_spec(layer, (1, B_W)),
            state,
        ],
        out_specs=[
            tok(A_W), tok(B_W), tok(B_W), tok(B_W),
            pl.BlockSpec((None, n_c, N_HEADS, GROUP_W, GROUP_W), lambda b, t: (b, t, 0, 0, 0)),
            pl.BlockSpec((None, None, n_c, B_W), lambda b, t: (b, t, 0, 0)),
            state,
        ],
        out_shape=[
            half, half, half, half,
            jax.ShapeDtypeStruct((B, L // GLA_CHUNK, N_HEADS, GROUP_W, GROUP_W), BF16),
            jax.ShapeDtypeStruct((B, n_t, n_c, B_W), F32),
            jax.ShapeDtypeStruct((B, N_HEADS, GROUP_W, GROUP_W), F32),
        ],
        scratch_shapes=[pltpu.VMEM((N_HEADS, GROUP_W, GROUP_W), F32)],
        compiler_params=pltpu.CompilerParams(
            dimension_semantics=("arbitrary", "arbitrary"), vmem_limit_bytes=VMEM_LIMIT),
        name="mixer_forward",
    )(x, mods, gn.reshape(DEPTH, 1, D_MODEL), w_in, ws, bs, gv.reshape(DEPTH, 1, A_W),
      lbf.reshape(DEPTH, 1, B_W), lbb.reshape(DEPTH, 1, B_W), s0)


def _mix2_kernel(mod_ref, a_ref, og_ref, op_ref, qb_ref, ds_ref, dg_ref, go_ref,
                 wout_ref, s0_ref, o_ref, sb_ref, st_ref, *, tm):
    C = GLA_CHUNK
    n_c = tm // C
    t_idx = pl.program_id(1)

    @pl.when(t_idx == 0)
    def _():
        st_ref[...] = s0_ref[...]

    state = [st_ref[hd] for hd in range(N_HEADS)]
    o_rows = [None] * n_c
    for c in reversed(range(n_c)):
        rs = slice(c * C, (c + 1) * C)
        qb = qb_ref[rs, :]
        dg = dg_ref[c:c + 1, :]
        o_heads = []
        for hd in range(N_HEADS):
            hs = slice(hd * GROUP_W, (hd + 1) * GROUP_W)
            st = state[hd]
            o_heads.append(op_ref[rs, hs].astype(F32) + _dot_nt(qb[:, hs], st.astype(BF16)))
            state[hd] = dg[:, hs] * st + ds_ref[c, hd].astype(F32)
        o_rows[c] = o_heads

    for hd in range(N_HEADS):
        st_ref[hd] = state[hd]

    @pl.when(t_idx == pl.num_programs(1) - 1)
    def _():
        for hd in range(N_HEADS):
            sb_ref[hd] = state[hd]

    y = _dot(a_ref[...], wout_ref[0:A_W, :])
    for hd in range(N_HEADS):
        hs = slice(hd * GROUP_W, (hd + 1) * GROUP_W)
        o_h = jnp.concatenate([o_rows[c][hd] for c in range(n_c)], axis=0)
        on = _rms(o_h, go_ref[:, hs]) * og_ref[:, hs].astype(F32)
        y = y + _dot(on.astype(BF16), wout_ref[A_W + hd * GROUP_W:A_W + (hd + 1) * GROUP_W, :])
    o_ref[...] = mod_ref[5:6, :] * y


def _mix2(mods, mod_row, a, og, op, qb, ds, dg, go, w_out, s0, *, layer, tm):
    B, L, _ = a.shape
    n_t = L // tm
    n_c = tm // GLA_CHUNK
    tok = lambda w: pl.BlockSpec((None, tm, w), lambda b, t: (b, n_t - 1 - t, 0))
    state = pl.BlockSpec((None, N_HEADS, GROUP_W, GROUP_W), lambda b, t: (b, 0, 0, 0))
    return pl.pallas_call(
        functools.partial(_mix2_kernel, tm=tm),
        grid=(B, n_t),
        in_specs=[
            _mod_spec(layer, mod_row),
            tok(A_W), tok(B_W), tok(B_W), tok(B_W),
            pl.BlockSpec((None, n_c, N_HEADS, GROUP_W, GROUP_W), lambda b, t: (b, n_t - 1 - t, 0, 0, 0)),
            pl.BlockSpec((None, None, n_c, B_W), lambda b, t: (b, n_t - 1 - t, 0, 0)),
            _layer_spec(layer, (1, B_W)),
            _layer_spec(layer, (D_MODEL, D_MODEL)),
            state,
        ],
        out_specs=[tok(D_MODEL), state],
        out_shape=[
            jax.ShapeDtypeStruct((B, L, D_MODEL), F32),
            jax.ShapeDtypeStruct((B, N_HEADS, GROUP_W, GROUP_W), F32),
        ],
        scratch_shapes=[pltpu.VMEM((N_HEADS, GROUP_W, GROUP_W), F32)],
        compiler_params=pltpu.CompilerParams(
            dimension_semantics=("arbitrary", "arbitrary"), vmem_limit_bytes=VMEM_LIMIT),
        name="mixer_backward",
    )(mods, a, og, op, qb, ds, dg, go.reshape(DEPTH, 1, B_W), w_out, s0)


def _grid_pos_embed(rows):
    quarter = D_MODEL // 4
    freq = POS_THETA ** (-jnp.arange(quarter, dtype=F32) / quarter)

    def axis_embed(n):
        a = jnp.arange(n).astype(F32)[:, None] * freq[None, :]
        return jnp.concatenate([jnp.sin(a), jnp.cos(a)], axis=-1)

    row_e = jnp.repeat(axis_embed(rows), GRID_W, axis=0)
    col_e = jnp.tile(axis_embed(GRID_W), (rows, 1))
    return jnp.concatenate([row_e, col_e], axis=-1)


def _lower_bounds(p):
    cum = jnp.cumsum(jax.nn.softmax(p.astype(F32), axis=0), axis=0)
    return cum - cum[0:1]


def _tile(L, want):
    tm = min(L, want)
    assert L % tm == 0 and tm % GMLP_CHUNK == 0 and tm % GLA_CHUNK == 0
    return tm


def kernel(x, c, ctx, c_ctx, w_ada, b_ada, norm_ffn1_g, norm_mix_g, norm_ffn2_g, ffn1_w1, ffn1_w3,
           ffn1_w2, ffn2_w1, ffn2_w3, ffn2_w2, w_in, w_out, gmlp_ws, gmlp_bs, gmlp_norm_g,
           hgrn_lb_fwd, hgrn_lb_bwd, hgrn_norm_g, norm_final_g):
    B, L, _ = x.shape
    n_ctx = ctx.shape[1]
    assert B + 1 <= MOD_ROWS
    tm_lat = _tile(L, 512)
    tm_mix2 = _tile(L, 1024)
    tm_ffn = _tile(L, 1024)
    tm_ctx = _tile(n_ctx, 512)
    tm_ctx_ffn = _tile(B * n_ctx, 1024)

    cond = jnp.concatenate([c, c_ctx[None], jnp.zeros((MOD_ROWS - B - 1, D_MODEL), F32)], axis=0)
    mods = _modulation(cond, w_ada, b_ada).reshape(DEPTH, MOD_ROWS, N_MOD, D_MODEL)
    lat_row = lambda b: b
    ctx_row = lambda b: B

    pos = _grid_pos_embed(L // GRID_W)
    lbs_f = _lower_bounds(hgrn_lb_fwd)
    lbs_b = _lower_bounds(hgrn_lb_bwd)
    bs_full = jnp.broadcast_to(gmlp_bs[..., None], gmlp_bs.shape + (GROUP_W,))
    ffn1 = [w.astype(BF16) for w in (ffn1_w1, ffn1_w3, ffn1_w2)]
    ffn2 = [w.astype(BF16) for w in (ffn2_w1, ffn2_w3, ffn2_w2)]
    zero_state = jnp.zeros((B, N_HEADS, GROUP_W, GROUP_W), F32)
    mix1 = functools.partial(_mix1, gn=norm_mix_g, w_in=w_in.astype(BF16), ws=gmlp_ws.astype(BF16),
                             bs=bs_full, gv=gmlp_norm_g, lbf=lbs_f, lbb=lbs_b)
    mix2 = functools.partial(_mix2, go=hgrn_norm_g, w_out=w_out.astype(BF16))

    def ctx_ffn(cx, g, w1, w3, w2, delta=None, **kw):
        flat = lambda a: None if a is None else a.reshape(1, B * n_ctx, D_MODEL)
        out = _ffn(flat(cx), mods, ctx_row, g, w1, w3, w2, tm=tm_ctx_ffn, delta=flat(delta), **kw)
        return out.reshape(B, n_ctx, D_MODEL)

    lat, cx = x, ctx
    for l in range(DEPTH):
        last = l == DEPTH - 1
        lat = _ffn(lat, mods, lat_row, norm_ffn1_g, *ffn1, layer=l, j0=0, tm=tm_ffn,
                   pos=pos if l == 0 else None)
        cx = ctx_ffn(cx, norm_ffn1_g, *ffn1, layer=l, j0=0)

        *parts_c, s_f = mix1(cx, mods, ctx_row, s0=zero_state, layer=l, tm=tm_ctx)
        cx_delta, s_b = mix2(mods, ctx_row, *parts_c, s0=zero_state, layer=l, tm=tm_ctx)
        *parts_l, dg_l, _ = mix1(lat, mods, lat_row, s0=s_f, layer=l, tm=tm_lat)
        dg_l = dg_l.reshape(B, L // tm_mix2, tm_mix2 // GLA_CHUNK, B_W)
        lat_delta, _ = mix2(mods, lat_row, *parts_l, dg_l, s0=s_b, layer=l, tm=tm_mix2)

        lat = _ffn(lat, mods, lat_row, norm_ffn2_g, *ffn2, layer=l, j0=6, tm=tm_ffn, delta=lat_delta,
                   gfin=norm_final_g if last else None)
        if not last:
            cx = ctx_ffn(cx, norm_ffn2_g, *ffn2, delta=cx_delta, layer=l, j0=6)
    return lat
```

```python
import functools

import jax
import jax.numpy as jnp
from jax import lax
from jax.experimental import pallas as pl
from jax.experimental.pallas import tpu as pltpu

D_MODEL = 1024
D_FF = 2816
DEPTH = 4
GRID_W = 64
A_W = 512
B_W = 512
IN_W = 2 * A_W + 5 * B_W
N_GROUPS = 4
GROUP_W = 128
GMLP_CHUNK = 128
N_HEADS = 4
N_MOD = 9
MOD_ROWS = 16
MOD_COLS = 3 * D_MODEL
EPS = 1e-6
POS_THETA = 10000.0

GLA_CHUNK = 64
GLA_BLOCK = 16
LOG_F_MIN = -10.0
SIDE_W = 256
FF_SPLITS = ((0, 768), (768, 768), (1536, 768), (2304, 512))

V7X_VMEM_BYTES = 64 * 1024 * 1024
VMEM_LIMIT = V7X_VMEM_BYTES - 8 * 1024 * 1024

F32 = jnp.float32
BF16 = jnp.bfloat16


def _silu(x):
    hx = 0.5 * x
    return hx + hx * jnp.tanh(hx)


def _gelu_tanh(x):
    hx = 0.5 * x
    return hx + hx * jnp.tanh(x * (0.7978845608028654 + 0.035677408136300125 * (x * x)))


def _rms(x, g):
    return x * lax.rsqrt(jnp.mean(x * x, axis=-1, keepdims=True) + EPS) * g


def _modulated_rms(x, g, shift, scale):
    return x * lax.rsqrt(jnp.mean(x * x, axis=-1, keepdims=True) + EPS) * (g * (1.0 + scale)) + shift


def _dot(a, b):
    return jnp.dot(a, b, preferred_element_type=F32)


def _dot_nt(a, b):
    return lax.dot_general(a, b, (((1,), (1,)), ((), ())), preferred_element_type=F32)


def _dot_tn(a, b):
    return lax.dot_general(a, b, (((0,), (0,)), ((), ())), preferred_element_type=F32)


def _layer_spec(layer, shape):
    nd = len(shape)
    return pl.BlockSpec((None,) + shape, lambda *_: (layer,) + (0,) * nd, pipeline_mode=pl.Buffered(1))


def _mod_spec(layer, mod_row):
    return pl.BlockSpec((None, None, N_MOD, D_MODEL), lambda b, t: (layer, mod_row(b), 0, 0))


def _mod_kernel(cond_ref, w_ref, b_ref, o_ref):
    s = _silu(cond_ref[...])
    o_ref[...] = jnp.dot(s, w_ref[...], preferred_element_type=F32,
                         precision=lax.Precision.HIGHEST) + b_ref[...]


def _modulation(cond, w_ada, b_ada):
    n_out = N_MOD * D_MODEL
    tn = MOD_COLS
    return pl.pallas_call(
        _mod_kernel,
        grid=(DEPTH, n_out // tn),
        in_specs=[
            pl.BlockSpec((MOD_ROWS, D_MODEL), lambda l, j: (0, 0)),
            pl.BlockSpec((None, D_MODEL, tn), lambda l, j: (l, 0, j)),
            pl.BlockSpec((None, 1, tn), lambda l, j: (l, 0, j)),
        ],
        out_specs=pl.BlockSpec((None, MOD_ROWS, tn), lambda l, j: (l, 0, j)),
        out_shape=jax.ShapeDtypeStruct((DEPTH, MOD_ROWS, n_out), F32),
        compiler_params=pltpu.CompilerParams(
            dimension_semantics=("arbitrary", "arbitrary"), vmem_limit_bytes=VMEM_LIMIT),
        name="adaln_modulation",
    )(cond, w_ada, b_ada.reshape(DEPTH, 1, n_out))


def _ffn_kernel(*refs, j0, add_pos, final):
    refs = list(refs)
    x_ref = refs.pop(0)
    pos_ref = refs.pop(0) if add_pos else None
    mod_ref, g_ref, w1_ref, w3_ref, w2_ref = refs[:5]
    refs = refs[5:]
    gfin_ref = refs.pop(0) if final else None
    o_ref = refs.pop(0)

    x = x_ref[...]
    if add_pos:
        x = x + pos_ref[...]
    shift = mod_ref[j0:j0 + 1, :]
    scale = mod_ref[j0 + 1:j0 + 2, :]
    gate = mod_ref[j0 + 2:j0 + 3, :]
    h = _modulated_rms(x, g_ref[...], shift, scale).astype(BF16)
    y = None
    for s, n in FF_SPLITS:
        a = _dot(h, w1_ref[:, s:s + n])
        b = _dot(h, w3_ref[:, s:s + n])
        p = (_silu(a) * b).astype(BF16)
        yc = _dot(p, w2_ref[s:s + n, :])
        y = yc if y is None else y + yc
    out = x + (0.5 * gate) * y
    if final:
        out = _rms(out, gfin_ref[...])
    o_ref[...] = out


def _ffn(x, mods, mod_row, g, w1, w3, w2, *, layer, j0, tm, pos=None, gfin=None):
    B, L, _ = x.shape
    add_pos, final = pos is not None, gfin is not None
    tok = pl.BlockSpec((None, tm, D_MODEL), lambda b, t: (b, t, 0))
    in_specs, args = [tok], [x]
    if add_pos:
        in_specs.append(pl.BlockSpec((tm, D_MODEL), lambda b, t: (t, 0)))
        args.append(pos)
    in_specs += [
        _mod_spec(layer, mod_row),
        _layer_spec(layer, (1, D_MODEL)),
        _layer_spec(layer, (D_MODEL, D_FF)),
        _layer_spec(layer, (D_MODEL, D_FF)),
        _layer_spec(layer, (D_FF, D_MODEL)),
    ]
    args += [mods, g.reshape(DEPTH, 1, D_MODEL), w1, w3, w2]
    if final:
        in_specs.append(pl.BlockSpec((1, D_MODEL), lambda b, t: (0, 0)))
        args.append(gfin.reshape(1, D_MODEL))
    return pl.pallas_call(
        functools.partial(_ffn_kernel, j0=j0, add_pos=add_pos, final=final),
        grid=(B, L // tm),
        in_specs=in_specs,
        out_specs=tok,
        out_shape=jax.ShapeDtypeStruct(x.shape, F32),
        compiler_params=pltpu.CompilerParams(
            dimension_semantics=("parallel", "parallel"), vmem_limit_bytes=VMEM_LIMIT),
        name="adaln_swiglu",
    )(*args)


def _forget(f_raw, lb):
    t = jnp.tanh(0.5 * f_raw)
    half = 0.5 * (1.0 - lb)
    k = half - half * t
    log_f = jnp.maximum(jnp.log((lb + half) + half * t), LOG_F_MIN)
    return k, log_f


def _cumsum_rows(g, tri2):
    hi = g.astype(BF16)
    lo = (g - hi.astype(F32)).astype(BF16)
    return _dot(tri2, jnp.concatenate([hi, lo], axis=0))


def _ref_rows(a, first, period):
    n = a.shape[0] // period
    parts = [jnp.broadcast_to(a[first + j * period:first + j * period + 1, :], (period, a.shape[1]))
             for j in range(n)]
    return parts[0] if n == 1 else jnp.concatenate(parts, axis=0)


def _gla_levels():
    levels = []
    m = GLA_BLOCK
    while m < GLA_CHUNK:
        levels.append(m)
        m *= 2
    return tuple(levels)


def _interleave_rows(first, second, m):
    C = first.shape[0]
    parts = [(first if (i // m) % 2 == 0 else second)[i:i + m] for i in range(0, C, m)]
    return jnp.concatenate(parts, axis=0)


def _score_operands(q, k, b, *, forward):
    half = GLA_BLOCK // 2
    d = b - _ref_rows(b, half - 1 if forward else half, GLA_BLOCK)
    if not forward:
        d = -d
    diag = ((q * jnp.exp(d)).astype(BF16), (k * jnp.exp(-d)).astype(BF16))
    levels = []
    for m in _gla_levels():
        d = b - _ref_rows(b, m - 1 if forward else m, 2 * m)
        if forward:
            qk, expo = _interleave_rows(k, q, m), _interleave_rows(-d, d, m)
        else:
            qk, expo = _interleave_rows(q, k, m), _interleave_rows(-d, d, m)
        levels.append((qk * jnp.exp(expo)).astype(BF16))
    return diag, levels


def _level_operands(lf, lb, m):
    zf, zb = jnp.zeros_like(lf), jnp.zeros_like(lb)
    q_side = jnp.concatenate([_interleave_rows(zf, lf, m), _interleave_rows(lb, zb, m)], axis=1)
    k_side = jnp.concatenate([_interleave_rows(lf, zf, m), _interleave_rows(zb, lb, m)], axis=1)
    return q_side, k_side


def _mix1_kernel(x_ref, mod_ref, gn_ref, win_ref, ws_ref, bs_ref, gv_ref, lbf_ref, lbb_ref, s0_ref,
                 a_ref, og_ref, op_ref, qb_ref, ds_ref, dg_ref, sf_ref, st_ref, *, tm):
    C = GLA_CHUNK
    n_c = tm // C
    t_idx = pl.program_id(1)

    @pl.when(t_idx == 0)
    def _():
        st_ref[...] = s0_ref[...]

    state = [st_ref[hd] for hd in range(N_HEADS)]

    x = x_ref[...]
    shift = mod_ref[3:4, :]
    scale = mod_ref[4:5, :]
    h = _modulated_rms(x, gn_ref[...], shift, scale).astype(BF16)
    def proj(lo, width):
        return _dot(h, win_ref[:, lo:lo + width])

    c0 = 2 * A_W
    q_all = _silu(proj(c0, B_W))
    kf_all, gf_all = _forget(proj(c0 + B_W, B_W), lbf_ref[...])
    kb_all, gb_all = _forget(proj(c0 + 2 * B_W, B_W), lbb_ref[...])
    v_all = proj(c0 + 3 * B_W, B_W).astype(BF16)

    ti = lax.broadcasted_iota(jnp.int32, (C, C), 0)
    si = lax.broadcasted_iota(jnp.int32, (C, C), 1)
    tri = (si <= ti).astype(BF16)
    tri2 = jnp.concatenate([tri, tri], axis=1)
    shift_blk = GLA_BLOCK.bit_length() - 1
    same_blk = lax.shift_right_logical(ti, shift_blk) == lax.shift_right_logical(si, shift_blk)
    diag_f = same_blk & (si <= ti)
    diag_b = same_blk & (si >= ti)
    parent_masks = [
        None if 2 * m == C else (lax.shift_right_logical(ti, (2 * m).bit_length() - 1)
                                 == lax.shift_right_logical(si, (2 * m).bit_length() - 1))
        for m in _gla_levels()]

    g_cat = jnp.concatenate([gf_all, gb_all], axis=1)
    prefix = [_cumsum_rows(g_cat[c * C:(c + 1) * C, :], tri2) for c in range(n_c)]

    def prep(c):
        rs = slice(c * C, (c + 1) * C)
        q = q_all[rs, :]
        kf = kf_all[rs, :]
        kb = kb_all[rs, :]
        gb = gb_all[rs, :]
        cs = prefix[c]
        bf = cs[:, :B_W]
        cb = cs[:, B_W:]
        eb = cb - gb
        diag_f_ops, lev_f = _score_operands(q, kf, bf, forward=True)
        diag_b_ops, lev_b = _score_operands(q, kb, eb, forward=False)
        bf_end = bf[C - 1:C, :]
        gb_tot = cb[C - 1:C, :]
        qb_ref[rs, :] = (q * jnp.exp(gb_tot - eb)).astype(BF16)
        dg_ref[c:c + 1, :] = jnp.exp(gb_tot)
        return dict(
            c=c, rs=rs, v=v_all[rs, :], diag_f=diag_f_ops, diag_b=diag_b_ops, lev_f=lev_f, lev_b=lev_b,
            q_in=(q * jnp.exp(bf)).astype(BF16),
            k_out_f=(kf * jnp.exp(bf_end - bf)).astype(BF16),
            k_out_b=(kb * jnp.exp(eb)).astype(BF16),
            dec_f=jnp.exp(bf_end))

    def raw_scores(p):
        out = []
        for hd in range(N_HEADS):
            hs = slice(hd * GROUP_W, (hd + 1) * GROUP_W)
            lv = [_dot_nt(*_level_operands(lf[:, hs], lb[:, hs], m))
                  for m, lf, lb in zip(_gla_levels(), p["lev_f"], p["lev_b"])]
            out.append((lv,
                        _dot_nt(p["diag_f"][0][:, hs], p["diag_f"][1][:, hs]),
                        _dot_nt(p["diag_b"][0][:, hs], p["diag_b"][1][:, hs])))
        return out

    def finish(p, raw):
        rs, v = p["rs"], p["v"]
        for hd in range(N_HEADS):
            hs = slice(hd * GROUP_W, (hd + 1) * GROUP_W)
            lv, df, db = raw[hd]
            scores = jnp.where(diag_f, df, 0.0) + jnp.where(diag_b, db, 0.0)
            for a, mk in zip(lv, parent_masks):
                scores = scores + (a if mk is None else jnp.where(mk, a, 0.0))
            st = state[hd]
            op_ref[rs, hs] = (_dot(scores.astype(BF16), v[:, hs])
                              + _dot_nt(p["q_in"][:, hs], st.astype(BF16))).astype(BF16)
            inc = _dot_tn(v[:, hs], jnp.concatenate([p["k_out_f"][:, hs], p["k_out_b"][:, hs]], axis=1))
            state[hd] = p["dec_f"][:, hs] * st + inc[:, :GROUP_W]
            ds_ref[p["c"], hd] = inc[:, GROUP_W:].astype(BF16)

    side_cols = [c0 + 4 * B_W, A_W, 0]
    side_jobs = [(lo + j * SIDE_W, SIDE_W) for lo in side_cols for j in range(B_W // SIDE_W)]
    side = []
    for c in range(max(n_c, len(side_jobs))):
        if c < n_c:
            p = prep(c)
            finish(p, raw_scores(p))
        if c < len(side_jobs):
            side.append(proj(*side_jobs[c]))
    per = B_W // SIDE_W
    og_ref[...] = _silu(jnp.concatenate(side[0:per], axis=1)).astype(BF16)
    z_v = jnp.concatenate(side[per:2 * per], axis=1)
    z_u = jnp.concatenate(side[2 * per:3 * per], axis=1)

    n_g = tm // GMLP_CHUNK
    for g in range(N_GROUPS):
        cu = slice(g * GROUP_W, (g + 1) * GROUP_W)
        vn = _rms(_gelu_tanh(z_v[:, cu]), gv_ref[:, cu]).astype(BF16)
        vn_wide = jnp.concatenate(
            [vn[cc * GMLP_CHUNK:(cc + 1) * GMLP_CHUNK, :] for cc in range(n_g)], axis=1)
        sv_wide = _dot(ws_ref[g], vn_wide)
        for cc in range(n_g):
            rows = slice(cc * GMLP_CHUNK, (cc + 1) * GMLP_CHUNK)
            sv = sv_wide[:, cc * GROUP_W:(cc + 1) * GROUP_W] + bs_ref[g]
            a_ref[rows, cu] = (_gelu_tanh(z_u[rows, cu]) * sv).astype(BF16)

    for hd in range(N_HEADS):
        st_ref[hd] = state[hd]

    @pl.when(t_idx == pl.num_programs(1) - 1)
    def _():
        for hd in range(N_HEADS):
            sf_ref[hd] = state[hd]


def _mix1(x, mods, mod_row, gn, w_in, ws, bs, gv, lbf, lbb, s0, *, layer, tm):
    B, L, _ = x.shape
    n_t = L // tm
    n_c = tm // GLA_CHUNK
    tok = lambda w: pl.BlockSpec((None, tm, w), lambda b, t: (b, t, 0))
    state = pl.BlockSpec((None, N_HEADS, GROUP_W, GROUP_W), lambda b, t: (b, 0, 0, 0))
    half = jax.ShapeDtypeStruct((B, L, B_W), BF16)
    return pl.pallas_call(
        functools.partial(_mix1_kernel, tm=tm),
        grid=(B, n_t),
        in_specs=[
            tok(D_MODEL),
            _mod_spec(layer, mod_row),
            _layer_spec(layer, (1, D_MODEL)),
            _layer_spec(layer, (D_MODEL, IN_W)),
            _layer_spec(layer, (N_GROUPS, GMLP_CHUNK, GMLP_CHUNK)),
            _layer_spec(layer, (N_GROUPS, GMLP_CHUNK, GROUP_W)),
            _layer_spec(layer, (1, A_W)),
            _layer_spec(layer, (1, B_W)),
            _layer_spec(layer, (1, B_W)),
            state,
        ],
        out_specs=[
            tok(A_W), tok(B_W), tok(B_W), tok(B_W),
            pl.BlockSpec((None, n_c, N_HEADS, GROUP_W, GROUP_W), lambda b, t: (b, t, 0, 0, 0)),
            pl.BlockSpec((None, None, n_c, B_W), lambda b, t: (b, t, 0, 0)),
            state,
        ],
        out_shape=[
            half, half, half, half,
            jax.ShapeDtypeStruct((B, L // GLA_CHUNK, N_HEADS, GROUP_W, GROUP_W), BF16),
            jax.ShapeDtypeStruct((B, n_t, n_c, B_W), F32),
            jax.ShapeDtypeStruct((B, N_HEADS, GROUP_W, GROUP_W), F32),
        ],
        scratch_shapes=[pltpu.VMEM((N_HEADS, GROUP_W, GROUP_W), F32)],
        compiler_params=pltpu.CompilerParams(
            dimension_semantics=("arbitrary", "arbitrary"), vmem_limit_bytes=VMEM_LIMIT),
        name="mixer_forward",
    )(x, mods, gn.reshape(DEPTH, 1, D_MODEL), w_in, ws, bs, gv.reshape(DEPTH, 1, A_W),
      lbf.reshape(DEPTH, 1, B_W), lbb.reshape(DEPTH, 1, B_W), s0)


def _mix2_kernel(x_ref, mod_ref, a_ref, og_ref, op_ref, qb_ref, ds_ref, dg_ref, go_ref,
                 wout_ref, s0_ref, o_ref, sb_ref, st_ref, *, tm):
    C = GLA_CHUNK
    n_c = tm // C
    t_idx = pl.program_id(1)

    @pl.when(t_idx == 0)
    def _():
        st_ref[...] = s0_ref[...]

    state = [st_ref[hd] for hd in range(N_HEADS)]
    o_rows = [None] * n_c
    for c in reversed(range(n_c)):
        rs = slice(c * C, (c + 1) * C)
        qb = qb_ref[rs, :]
        dg = dg_ref[c:c + 1, :]
        o_heads = []
        for hd in range(N_HEADS):
            hs = slice(hd * GROUP_W, (hd + 1) * GROUP_W)
            st = state[hd]
            o_heads.append(op_ref[rs, hs].astype(F32) + _dot_nt(qb[:, hs], st.astype(BF16)))
            state[hd] = dg[:, hs] * st + ds_ref[c, hd].astype(F32)
        o_rows[c] = o_heads

    for hd in range(N_HEADS):
        st_ref[hd] = state[hd]

    @pl.when(t_idx == pl.num_programs(1) - 1)
    def _():
        for hd in range(N_HEADS):
            sb_ref[hd] = state[hd]

    y = _dot(a_ref[...], wout_ref[0:A_W, :])
    for hd in range(N_HEADS):
        hs = slice(hd * GROUP_W, (hd + 1) * GROUP_W)
        o_h = jnp.concatenate([o_rows[c][hd] for c in range(n_c)], axis=0)
        on = _rms(o_h, go_ref[:, hs]) * og_ref[:, hs].astype(F32)
        y = y + _dot(on.astype(BF16), wout_ref[A_W + hd * GROUP_W:A_W + (hd + 1) * GROUP_W, :])
    o_ref[...] = x_ref[...] + mod_ref[5:6, :] * y


def _mix2(x, mods, mod_row, a, og, op, qb, ds, dg, go, w_out, s0, *, layer, tm):
    B, L, _ = x.shape
    n_t = L // tm
    n_c = tm // GLA_CHUNK
    tok = lambda w: pl.BlockSpec((None, tm, w), lambda b, t: (b, n_t - 1 - t, 0))
    state = pl.BlockSpec((None, N_HEADS, GROUP_W, GROUP_W), lambda b, t: (b, 0, 0, 0))
    return pl.pallas_call(
        functools.partial(_mix2_kernel, tm=tm),
        grid=(B, n_t),
        in_specs=[
            tok(D_MODEL),
            _mod_spec(layer, mod_row),
            tok(A_W), tok(B_W), tok(B_W), tok(B_W),
            pl.BlockSpec((None, n_c, N_HEADS, GROUP_W, GROUP_W), lambda b, t: (b, n_t - 1 - t, 0, 0, 0)),
            pl.BlockSpec((None, None, n_c, B_W), lambda b, t: (b, n_t - 1 - t, 0, 0)),
            _layer_spec(layer, (1, B_W)),
            _layer_spec(layer, (D_MODEL, D_MODEL)),
            state,
        ],
        out_specs=[tok(D_MODEL), state],
        out_shape=[
            jax.ShapeDtypeStruct(x.shape, F32),
            jax.ShapeDtypeStruct((B, N_HEADS, GROUP_W, GROUP_W), F32),
        ],
        scratch_shapes=[pltpu.VMEM((N_HEADS, GROUP_W, GROUP_W), F32)],
        compiler_params=pltpu.CompilerParams(
            dimension_semantics=("arbitrary", "arbitrary"), vmem_limit_bytes=VMEM_LIMIT),
        name="mixer_backward",
    )(x, mods, a, og, op, qb, ds, dg, go.reshape(DEPTH, 1, B_W), w_out, s0)


def _grid_pos_embed(rows):
    quarter = D_MODEL // 4
    freq = POS_THETA ** (-jnp.arange(quarter, dtype=F32) / quarter)

    def axis_embed(n):
        a = jnp.arange(n).astype(F32)[:, None] * freq[None, :]
        return jnp.concatenate([jnp.sin(a), jnp.cos(a)], axis=-1)

    row_e = jnp.repeat(axis_embed(rows), GRID_W, axis=0)
    col_e = jnp.tile(axis_embed(GRID_W), (rows, 1))
    return jnp.concatenate([row_e, col_e], axis=-1)


def _lower_bounds(p):
    cum = jnp.cumsum(jax.nn.softmax(p.astype(F32), axis=0), axis=0)
    return cum - cum[0:1]


def _tile(L, want):
    tm = min(L, want)
    assert L % tm == 0 and tm % GMLP_CHUNK == 0 and tm % GLA_CHUNK == 0
    return tm


def kernel(x, c, ctx, c_ctx, w_ada, b_ada, norm_ffn1_g, norm_mix_g, norm_ffn2_g, ffn1_w1, ffn1_w3,
           ffn1_w2, ffn2_w1, ffn2_w3, ffn2_w2, w_in, w_out, gmlp_ws, gmlp_bs, gmlp_norm_g,
           hgrn_lb_fwd, hgrn_lb_bwd, hgrn_norm_g, norm_final_g):
    B, L, _ = x.shape
    n_ctx = ctx.shape[1]
    assert B + 1 <= MOD_ROWS
    tm_lat = _tile(L, 512)
    tm_mix2 = _tile(L, 1024)
    tm_ffn = _tile(L, 1024)
    tm_ctx = _tile(n_ctx, 512)
    tm_ctx_ffn = _tile(B * n_ctx, 1024)

    cond = jnp.concatenate([c, c_ctx[None], jnp.zeros((MOD_ROWS - B - 1, D_MODEL), F32)], axis=0)
    mods = _modulation(cond, w_ada, b_ada).reshape(DEPTH, MOD_ROWS, N_MOD, D_MODEL)
    lat_row = lambda b: b
    ctx_row = lambda b: B

    pos = _grid_pos_embed(L // GRID_W)
    lbs_f = _lower_bounds(hgrn_lb_fwd)
    lbs_b = _lower_bounds(hgrn_lb_bwd)
    bs_full = jnp.broadcast_to(gmlp_bs[..., None], gmlp_bs.shape + (GROUP_W,))
    ffn1 = [w.astype(BF16) for w in (ffn1_w1, ffn1_w3, ffn1_w2)]
    ffn2 = [w.astype(BF16) for w in (ffn2_w1, ffn2_w3, ffn2_w2)]
    zero_state = jnp.zeros((B, N_HEADS, GROUP_W, GROUP_W), F32)
    mix1 = functools.partial(_mix1, gn=norm_mix_g, w_in=w_in.astype(BF16), ws=gmlp_ws.astype(BF16),
                             bs=bs_full, gv=gmlp_norm_g, lbf=lbs_f, lbb=lbs_b)
    mix2 = functools.partial(_mix2, go=hgrn_norm_g, w_out=w_out.astype(BF16))

    def ctx_ffn(cx, g, w1, w3, w2, **kw):
        flat = cx.reshape(1, B * n_ctx, D_MODEL)
        return _ffn(flat, mods, ctx_row, g, w1, w3, w2, tm=tm_ctx_ffn, **kw).reshape(B, n_ctx, D_MODEL)

    lat, cx = x, ctx
    for l in range(DEPTH):
        last = l == DEPTH - 1
        lat = _ffn(lat, mods, lat_row, norm_ffn1_g, *ffn1, layer=l, j0=0, tm=tm_ffn,
                   pos=pos if l == 0 else None)
        cx = ctx_ffn(cx, norm_ffn1_g, *ffn1, layer=l, j0=0)

        *parts_c, s_f = mix1(cx, mods, ctx_row, s0=zero_state, layer=l, tm=tm_ctx)
        cx_mixed, s_b = mix2(cx, mods, ctx_row, *parts_c, s0=zero_state, layer=l, tm=tm_ctx)
        *parts_l, dg_l, _ = mix1(lat, mods, lat_row, s0=s_f, layer=l, tm=tm_lat)
        dg_l = dg_l.reshape(B, L // tm_mix2, tm_mix2 // GLA_CHUNK, B_W)
        lat, _ = mix2(lat, mods, lat_row, *parts_l, dg_l, s0=s_b, layer=l, tm=tm_mix2)

        lat = _ffn(lat, mods, lat_row, norm_ffn2_g, *ffn2, layer=l, j0=6, tm=tm_ffn,
                   gfin=norm_final_g if last else None)
        if not last:
            cx = ctx_ffn(cx_mixed, norm_ffn2_g, *ffn2, layer=l, j0=6)
    return lat
```

```python
import functools

import jax
import jax.numpy as jnp
from jax import lax
from jax.experimental import pallas as pl
from jax.experimental.pallas import tpu as pltpu

D_MODEL = 1024
D_FF = 2816
DEPTH = 4
GRID_W = 64
A_W = 512
B_W = 512
IN_W = 2 * A_W + 5 * B_W
N_GROUPS = 4
GROUP_W = 128
GMLP_CHUNK = 128
N_HEADS = 4
N_MOD = 9
MOD_ROWS = 16
MOD_COLS = 3 * D_MODEL
EPS = 1e-6
POS_THETA = 10000.0

GLA_CHUNK = 64
GLA_BLOCK = 16
LOG_F_MIN = -10.0
X_SLOTS = 3
SIDE_W = 256
FF_SPLITS = ((0, 768), (768, 768), (1536, 768), (2304, 512))

V7X_VMEM_BYTES = 64 * 1024 * 1024
VMEM_LIMIT = V7X_VMEM_BYTES - 8 * 1024 * 1024

F32 = jnp.float32
BF16 = jnp.bfloat16


def _silu(x):
    hx = 0.5 * x
    return hx + hx * jnp.tanh(hx)


def _gelu_tanh(x):
    hx = 0.5 * x
    return hx + hx * jnp.tanh(x * (0.7978845608028654 + 0.035677408136300125 * (x * x)))


def _rms(x, g):
    return x * lax.rsqrt(jnp.mean(x * x, axis=-1, keepdims=True) + EPS) * g


def _modulated_rms(x, g, shift, scale):
    return x * lax.rsqrt(jnp.mean(x * x, axis=-1, keepdims=True) + EPS) * (g * (1.0 + scale)) + shift


def _dot(a, b):
    return jnp.dot(a, b, preferred_element_type=F32)


def _dot_nt(a, b):
    return lax.dot_general(a, b, (((1,), (1,)), ((), ())), preferred_element_type=F32)


def _dot_tn(a, b):
    return lax.dot_general(a, b, (((0,), (0,)), ((), ())), preferred_element_type=F32)


def _layer_spec(layer, shape):
    nd = len(shape)
    return pl.BlockSpec((None,) + shape, lambda *_: (layer,) + (0,) * nd, pipeline_mode=pl.Buffered(1))


def _mod_spec(layer, mod_row):
    return pl.BlockSpec((None, None, N_MOD, D_MODEL), lambda b, t: (layer, mod_row(b), 0, 0))


def _mod_kernel(cond_ref, w_ref, b_ref, o_ref):
    s = _silu(cond_ref[...])
    o_ref[...] = jnp.dot(s, w_ref[...], preferred_element_type=F32,
                         precision=lax.Precision.HIGHEST) + b_ref[...]


def _modulation(cond, w_ada, b_ada):
    n_out = N_MOD * D_MODEL
    tn = MOD_COLS
    return pl.pallas_call(
        _mod_kernel,
        grid=(DEPTH, n_out // tn),
        in_specs=[
            pl.BlockSpec((MOD_ROWS, D_MODEL), lambda l, j: (0, 0)),
            pl.BlockSpec((None, D_MODEL, tn), lambda l, j: (l, 0, j)),
            pl.BlockSpec((None, 1, tn), lambda l, j: (l, 0, j)),
        ],
        out_specs=pl.BlockSpec((None, MOD_ROWS, tn), lambda l, j: (l, 0, j)),
        out_shape=jax.ShapeDtypeStruct((DEPTH, MOD_ROWS, n_out), F32),
        compiler_params=pltpu.CompilerParams(
            dimension_semantics=("arbitrary", "arbitrary"), vmem_limit_bytes=VMEM_LIMIT),
        name="adaln_modulation",
    )(cond, w_ada, b_ada.reshape(DEPTH, 1, n_out))


def _ffn_kernel(*refs, j0, add_pos, final):
    refs = list(refs)
    x_ref = refs.pop(0)
    pos_ref = refs.pop(0) if add_pos else None
    mod_ref, g_ref, w1_ref, w3_ref, w2_ref = refs[:5]
    refs = refs[5:]
    gfin_ref = refs.pop(0) if final else None
    o_ref = refs.pop(0)

    x = x_ref[...]
    if add_pos:
        x = x + pos_ref[...]
    shift = mod_ref[j0:j0 + 1, :]
    scale = mod_ref[j0 + 1:j0 + 2, :]
    gate = mod_ref[j0 + 2:j0 + 3, :]
    h = _modulated_rms(x, g_ref[...], shift, scale).astype(BF16)
    y = None
    for s, n in FF_SPLITS:
        a = _dot(h, w1_ref[:, s:s + n])
        b = _dot(h, w3_ref[:, s:s + n])
        p = (_silu(a) * b).astype(BF16)
        yc = _dot(p, w2_ref[s:s + n, :])
        y = yc if y is None else y + yc
    out = x + (0.5 * gate) * y
    if final:
        out = _rms(out, gfin_ref[...])
    o_ref[...] = out


def _ffn(x, mods, mod_row, g, w1, w3, w2, *, layer, j0, tm, pos=None, gfin=None):
    B, L, _ = x.shape
    add_pos, final = pos is not None, gfin is not None
    tok = pl.BlockSpec((None, tm, D_MODEL), lambda b, t: (b, t, 0))
    in_specs, args = [tok], [x]
    if add_pos:
        in_specs.append(pl.BlockSpec((tm, D_MODEL), lambda b, t: (t, 0)))
        args.append(pos)
    in_specs += [
        _mod_spec(layer, mod_row),
        _layer_spec(layer, (1, D_MODEL)),
        _layer_spec(layer, (D_MODEL, D_FF)),
        _layer_spec(layer, (D_MODEL, D_FF)),
        _layer_spec(layer, (D_FF, D_MODEL)),
    ]
    args += [mods, g.reshape(DEPTH, 1, D_MODEL), w1, w3, w2]
    if final:
        in_specs.append(pl.BlockSpec((1, D_MODEL), lambda b, t: (0, 0)))
        args.append(gfin.reshape(1, D_MODEL))
    return pl.pallas_call(
        functools.partial(_ffn_kernel, j0=j0, add_pos=add_pos, final=final),
        grid=(B, L // tm),
        in_specs=in_specs,
        out_specs=tok,
        out_shape=jax.ShapeDtypeStruct(x.shape, F32),
        compiler_params=pltpu.CompilerParams(
            dimension_semantics=("parallel", "parallel"), vmem_limit_bytes=VMEM_LIMIT),
        name="adaln_swiglu",
    )(*args)


def _forget(f_raw, lb):
    t = jnp.tanh(0.5 * f_raw)
    half = 0.5 * (1.0 - lb)
    k = half - half * t
    log_f = jnp.maximum(jnp.log((lb + half) + half * t), LOG_F_MIN)
    return k, log_f


def _cumsum_rows(g, tri2):
    hi = g.astype(BF16)
    lo = (g - hi.astype(F32)).astype(BF16)
    return _dot(tri2, jnp.concatenate([hi, lo], axis=0))


def _ref_rows(a, first, period):
    n = a.shape[0] // period
    parts = [jnp.broadcast_to(a[first + j * period:first + j * period + 1, :], (period, a.shape[1]))
             for j in range(n)]
    return parts[0] if n == 1 else jnp.concatenate(parts, axis=0)


def _gla_levels():
    levels = []
    m = GLA_BLOCK
    while m < GLA_CHUNK:
        levels.append(m)
        m *= 2
    return tuple(levels)


def _interleave_rows(first, second, m):
    C = first.shape[0]
    parts = [(first if (i // m) % 2 == 0 else second)[i:i + m] for i in range(0, C, m)]
    return jnp.concatenate(parts, axis=0)


def _score_operands(q, k, b, *, forward):
    half = GLA_BLOCK // 2
    d = b - _ref_rows(b, half - 1 if forward else half, GLA_BLOCK)
    if not forward:
        d = -d
    diag = ((q * jnp.exp(d)).astype(BF16), (k * jnp.exp(-d)).astype(BF16))
    levels = []
    for m in _gla_levels():
        d = b - _ref_rows(b, m - 1 if forward else m, 2 * m)
        if forward:
            qk, expo = _interleave_rows(k, q, m), _interleave_rows(-d, d, m)
        else:
            qk, expo = _interleave_rows(q, k, m), _interleave_rows(-d, d, m)
        levels.append((qk * jnp.exp(expo)).astype(BF16))
    return diag, levels


def _level_operands(lf, lb, m):
    zf, zb = jnp.zeros_like(lf), jnp.zeros_like(lb)
    q_side = jnp.concatenate([_interleave_rows(zf, lf, m), _interleave_rows(lb, zb, m)], axis=1)
    k_side = jnp.concatenate([_interleave_rows(lf, zf, m), _interleave_rows(zb, lb, m)], axis=1)
    return q_side, k_side


def _mix1_kernel(x_ref, mod_ref, gn_ref, win_ref, ws_ref, bs_ref, gv_ref, lbf_ref, lbb_ref, s0_ref,
                 a_ref, og_ref, op_ref, qb_ref, ds_ref, dg_ref, sf_ref, st_ref, *, tm):
    C = GLA_CHUNK
    n_c = tm // C
    t_idx = pl.program_id(1)

    @pl.when(t_idx == 0)
    def _():
        st_ref[...] = s0_ref[...]

    state = [st_ref[hd] for hd in range(N_HEADS)]

    x = x_ref[...]
    shift = mod_ref[3:4, :]
    scale = mod_ref[4:5, :]
    h = _modulated_rms(x, gn_ref[...], shift, scale).astype(BF16)
    def proj(lo, width):
        return _dot(h, win_ref[:, lo:lo + width])

    c0 = 2 * A_W
    q_all = _silu(proj(c0, B_W))
    kf_all, gf_all = _forget(proj(c0 + B_W, B_W), lbf_ref[...])
    kb_all, gb_all = _forget(proj(c0 + 2 * B_W, B_W), lbb_ref[...])
    v_all = proj(c0 + 3 * B_W, B_W).astype(BF16)

    ti = lax.broadcasted_iota(jnp.int32, (C, C), 0)
    si = lax.broadcasted_iota(jnp.int32, (C, C), 1)
    tri = (si <= ti).astype(BF16)
    tri2 = jnp.concatenate([tri, tri], axis=1)
    shift_blk = GLA_BLOCK.bit_length() - 1
    same_blk = lax.shift_right_logical(ti, shift_blk) == lax.shift_right_logical(si, shift_blk)
    diag_f = same_blk & (si <= ti)
    diag_b = same_blk & (si >= ti)
    parent_masks = [
        None if 2 * m == C else (lax.shift_right_logical(ti, (2 * m).bit_length() - 1)
                                 == lax.shift_right_logical(si, (2 * m).bit_length() - 1))
        for m in _gla_levels()]

    g_cat = jnp.concatenate([gf_all, gb_all], axis=1)
    prefix = [_cumsum_rows(g_cat[c * C:(c + 1) * C, :], tri2) for c in range(n_c)]

    def prep(c):
        rs = slice(c * C, (c + 1) * C)
        q = q_all[rs, :]
        kf = kf_all[rs, :]
        kb = kb_all[rs, :]
        gb = gb_all[rs, :]
        cs = prefix[c]
        bf = cs[:, :B_W]
        cb = cs[:, B_W:]
        eb = cb - gb
        diag_f_ops, lev_f = _score_operands(q, kf, bf, forward=True)
        diag_b_ops, lev_b = _score_operands(q, kb, eb, forward=False)
        bf_end = bf[C - 1:C, :]
        gb_tot = cb[C - 1:C, :]
        qb_ref[rs, :] = (q * jnp.exp(gb_tot - eb)).astype(BF16)
        dg_ref[c:c + 1, :] = jnp.exp(gb_tot)
        return dict(
            c=c, rs=rs, v=v_all[rs, :], diag_f=diag_f_ops, diag_b=diag_b_ops, lev_f=lev_f, lev_b=lev_b,
            q_in=(q * jnp.exp(bf)).astype(BF16),
            k_out_f=(kf * jnp.exp(bf_end - bf)).astype(BF16),
            k_out_b=(kb * jnp.exp(eb)).astype(BF16),
            dec_f=jnp.exp(bf_end))

    def raw_scores(p):
        out = []
        for hd in range(N_HEADS):
            hs = slice(hd * GROUP_W, (hd + 1) * GROUP_W)
            lv = [_dot_nt(*_level_operands(lf[:, hs], lb[:, hs], m))
                  for m, lf, lb in zip(_gla_levels(), p["lev_f"], p["lev_b"])]
            out.append((lv,
                        _dot_nt(p["diag_f"][0][:, hs], p["diag_f"][1][:, hs]),
                        _dot_nt(p["diag_b"][0][:, hs], p["diag_b"][1][:, hs])))
        return out

    def finish(p, raw):
        rs, v = p["rs"], p["v"]
        for hd in range(N_HEADS):
            hs = slice(hd * GROUP_W, (hd + 1) * GROUP_W)
            lv, df, db = raw[hd]
            scores = jnp.where(diag_f, df, 0.0) + jnp.where(diag_b, db, 0.0)
            for a, mk in zip(lv, parent_masks):
                scores = scores + (a if mk is None else jnp.where(mk, a, 0.0))
            st = state[hd]
            op_ref[rs, hs] = (_dot(scores.astype(BF16), v[:, hs])
                              + _dot_nt(p["q_in"][:, hs], st.astype(BF16))).astype(BF16)
            inc = _dot_tn(v[:, hs], jnp.concatenate([p["k_out_f"][:, hs], p["k_out_b"][:, hs]], axis=1))
            state[hd] = p["dec_f"][:, hs] * st + inc[:, :GROUP_W]
            ds_ref[p["c"], hd] = inc[:, GROUP_W:].astype(BF16)

    side_cols = [c0 + 4 * B_W, A_W, 0]
    side_jobs = [(lo + j * SIDE_W, SIDE_W) for lo in side_cols for j in range(B_W // SIDE_W)]
    side = []
    for c in range(max(n_c, len(side_jobs))):
        if c < n_c:
            p = prep(c)
            finish(p, raw_scores(p))
        if c < len(side_jobs):
            side.append(proj(*side_jobs[c]))
    per = B_W // SIDE_W
    og_ref[...] = _silu(jnp.concatenate(side[0:per], axis=1)).astype(BF16)
    z_v = jnp.concatenate(side[per:2 * per], axis=1)
    z_u = jnp.concatenate(side[2 * per:3 * per], axis=1)

    n_g = tm // GMLP_CHUNK
    for g in range(N_GROUPS):
        cu = slice(g * GROUP_W, (g + 1) * GROUP_W)
        vn = _rms(_gelu_tanh(z_v[:, cu]), gv_ref[:, cu]).astype(BF16)
        vn_wide = jnp.concatenate(
            [vn[cc * GMLP_CHUNK:(cc + 1) * GMLP_CHUNK, :] for cc in range(n_g)], axis=1)
        sv_wide = _dot(ws_ref[g], vn_wide)
        for cc in range(n_g):
            rows = slice(cc * GMLP_CHUNK, (cc + 1) * GMLP_CHUNK)
            sv = sv_wide[:, cc * GROUP_W:(cc + 1) * GROUP_W] + bs_ref[g]
            a_ref[rows, cu] = (_gelu_tanh(z_u[rows, cu]) * sv).astype(BF16)

    for hd in range(N_HEADS):
        st_ref[hd] = state[hd]

    @pl.when(t_idx == pl.num_programs(1) - 1)
    def _():
        for hd in range(N_HEADS):
            sf_ref[hd] = state[hd]


def _mix1(x, mods, mod_row, gn, w_in, ws, bs, gv, lbf, lbb, s0, *, layer, tm):
    B, L, _ = x.shape
    n_t = L // tm
    n_c = tm // GLA_CHUNK
    tok = lambda w: pl.BlockSpec((None, tm, w), lambda b, t: (b, t, 0))
    state = pl.BlockSpec((None, N_HEADS, GROUP_W, GROUP_W), lambda b, t: (b, 0, 0, 0))
    half = jax.ShapeDtypeStruct((B, L, B_W), BF16)
    return pl.pallas_call(
        functools.partial(_mix1_kernel, tm=tm),
        grid=(B, n_t),
        in_specs=[
            tok(D_MODEL),
            _mod_spec(layer, mod_row),
            _layer_spec(layer, (1, D_MODEL)),
            _layer_spec(layer, (D_MODEL, IN_W)),
            _layer_spec(layer, (N_GROUPS, GMLP_CHUNK, GMLP_CHUNK)),
            _layer_spec(layer, (N_GROUPS, GMLP_CHUNK, GROUP_W)),
            _layer_spec(layer, (1, A_W)),
            _layer_spec(layer, (1, B_W)),
            _layer_spec(layer, (1, B_W)),
            state,
        ],
        out_specs=[
            tok(A_W), tok(B_W), tok(B_W), tok(B_W),
            pl.BlockSpec((None, n_c, N_HEADS, GROUP_W, GROUP_W), lambda b, t: (b, t, 0, 0, 0)),
            pl.BlockSpec((None, None, n_c, B_W), lambda b, t: (b, t, 0, 0)),
            state,
        ],
        out_shape=[
            half, half, half, half,
            jax.ShapeDtypeStruct((B, L // GLA_CHUNK, N_HEADS, GROUP_W, GROUP_W), BF16),
            jax.ShapeDtypeStruct((B, n_t, n_c, B_W), F32),
            jax.ShapeDtypeStruct((B, N_HEADS, GROUP_W, GROUP_W), F32),
        ],
        scratch_shapes=[pltpu.VMEM((N_HEADS, GROUP_W, GROUP_W), F32)],
        compiler_params=pltpu.CompilerParams(
            dimension_semantics=("arbitrary", "arbitrary"), vmem_limit_bytes=VMEM_LIMIT),
        name="mixer_forward",
    )(x, mods, gn.reshape(DEPTH, 1, D_MODEL), w_in, ws, bs, gv.reshape(DEPTH, 1, A_W),
      lbf.reshape(DEPTH, 1, B_W), lbb.reshape(DEPTH, 1, B_W), s0)


def _mix2_kernel(x_hbm, mod_ref, a_ref, og_ref, op_ref, qb_ref, ds_ref, dg_ref, go_ref,
                 wout_ref, s0_ref, o_ref, sb_ref, st_ref, x_buf, x_sem, *, tm, n_t, n_steps):
    C = GLA_CHUNK
    n_c = tm // C
    t_idx = pl.program_id(1)

    step = pl.program_id(0) * n_t + t_idx

    def x_copy(k):
        tile = n_t - 1 - k % n_t
        return pltpu.make_async_copy(x_hbm.at[k // n_t, pl.ds(tile * tm, tm), :],
                                     x_buf.at[k % X_SLOTS], x_sem.at[k % X_SLOTS])

    @pl.when(step == 0)
    def _():
        for k in range(min(X_SLOTS - 1, n_steps)):
            x_copy(k).start()

    @pl.when(step + (X_SLOTS - 1) < n_steps)
    def _():
        x_copy(step + (X_SLOTS - 1)).start()

    x_copy(step).wait()

    @pl.when(t_idx == 0)
    def _():
        st_ref[...] = s0_ref[...]

    state = [st_ref[hd] for hd in range(N_HEADS)]
    o_rows = [None] * n_c
    for c in reversed(range(n_c)):
        rs = slice(c * C, (c + 1) * C)
        qb = qb_ref[rs, :]
        dg = dg_ref[c:c + 1, :]
        o_heads = []
        for hd in range(N_HEADS):
            hs = slice(hd * GROUP_W, (hd + 1) * GROUP_W)
            st = state[hd]
            o_heads.append(op_ref[rs, hs].astype(F32) + _dot_nt(qb[:, hs], st.astype(BF16)))
            state[hd] = dg[:, hs] * st + ds_ref[c, hd].astype(F32)
        o_rows[c] = o_heads

    for hd in range(N_HEADS):
        st_ref[hd] = state[hd]

    @pl.when(t_idx == pl.num_programs(1) - 1)
    def _():
        for hd in range(N_HEADS):
            sb_ref[hd] = state[hd]

    y = _dot(a_ref[...], wout_ref[0:A_W, :])
    for hd in range(N_HEADS):
        hs = slice(hd * GROUP_W, (hd + 1) * GROUP_W)
        o_h = jnp.concatenate([o_rows[c][hd] for c in range(n_c)], axis=0)
        on = _rms(o_h, go_ref[:, hs]) * og_ref[:, hs].astype(F32)
        y = y + _dot(on.astype(BF16), wout_ref[A_W + hd * GROUP_W:A_W + (hd + 1) * GROUP_W, :])
    o_ref[...] = x_buf[step % X_SLOTS] + mod_ref[5:6, :] * y


def _mix2(x, mods, mod_row, a, og, op, qb, ds, dg, go, w_out, s0, *, layer, tm):
    B, L, _ = x.shape
    n_t = L // tm
    n_c = tm // GLA_CHUNK
    tok = lambda w: pl.BlockSpec((None, tm, w), lambda b, t: (b, n_t - 1 - t, 0))
    state = pl.BlockSpec((None, N_HEADS, GROUP_W, GROUP_W), lambda b, t: (b, 0, 0, 0))
    return pl.pallas_call(
        functools.partial(_mix2_kernel, tm=tm, n_t=n_t, n_steps=B * n_t),
        grid=(B, n_t),
        in_specs=[
            pl.BlockSpec(memory_space=pl.ANY),
            _mod_spec(layer, mod_row),
            tok(A_W), tok(B_W), tok(B_W), tok(B_W),
            pl.BlockSpec((None, n_c, N_HEADS, GROUP_W, GROUP_W), lambda b, t: (b, n_t - 1 - t, 0, 0, 0)),
            pl.BlockSpec((None, None, n_c, B_W), lambda b, t: (b, n_t - 1 - t, 0, 0)),
            _layer_spec(layer, (1, B_W)),
            _layer_spec(layer, (D_MODEL, D_MODEL)),
            state,
        ],
        out_specs=[tok(D_MODEL), state],
        out_shape=[
            jax.ShapeDtypeStruct(x.shape, F32),
            jax.ShapeDtypeStruct((B, N_HEADS, GROUP_W, GROUP_W), F32),
        ],
        scratch_shapes=[pltpu.VMEM((N_HEADS, GROUP_W, GROUP_W), F32),
                        pltpu.VMEM((X_SLOTS, tm, D_MODEL), F32),
                        pltpu.SemaphoreType.DMA((X_SLOTS,))],
        compiler_params=pltpu.CompilerParams(
            dimension_semantics=("arbitrary", "arbitrary"), vmem_limit_bytes=VMEM_LIMIT),
        name="mixer_backward",
    )(x, mods, a, og, op, qb, ds, dg, go.reshape(DEPTH, 1, B_W), w_out, s0)


def _grid_pos_embed(rows):
    quarter = D_MODEL // 4
    freq = POS_THETA ** (-jnp.arange(quarter, dtype=F32) / quarter)

    def axis_embed(n):
        a = jnp.arange(n).astype(F32)[:, None] * freq[None, :]
        return jnp.concatenate([jnp.sin(a), jnp.cos(a)], axis=-1)

    row_e = jnp.repeat(axis_embed(rows), GRID_W, axis=0)
    col_e = jnp.tile(axis_embed(GRID_W), (rows, 1))
    return jnp.concatenate([row_e, col_e], axis=-1)


def _lower_bounds(p):
    cum = jnp.cumsum(jax.nn.softmax(p.astype(F32), axis=0), axis=0)
    return cum - cum[0:1]


def _tile(L, want):
    tm = min(L, want)
    assert L % tm == 0 and tm % GMLP_CHUNK == 0 and tm % GLA_CHUNK == 0
    return tm


def kernel(x, c, ctx, c_ctx, w_ada, b_ada, norm_ffn1_g, norm_mix_g, norm_ffn2_g, ffn1_w1, ffn1_w3,
           ffn1_w2, ffn2_w1, ffn2_w3, ffn2_w2, w_in, w_out, gmlp_ws, gmlp_bs, gmlp_norm_g,
           hgrn_lb_fwd, hgrn_lb_bwd, hgrn_norm_g, norm_final_g):
    B, L, _ = x.shape
    n_ctx = ctx.shape[1]
    assert B + 1 <= MOD_ROWS
    tm_lat = _tile(L, 512)
    tm_mix2 = _tile(L, 1024)
    tm_ffn = _tile(L, 1024)
    tm_ctx = _tile(n_ctx, 512)
    tm_ctx_ffn = _tile(B * n_ctx, 1024)

    cond = jnp.concatenate([c, c_ctx[None], jnp.zeros((MOD_ROWS - B - 1, D_MODEL), F32)], axis=0)
    mods = _modulation(cond, w_ada, b_ada).reshape(DEPTH, MOD_ROWS, N_MOD, D_MODEL)
    lat_row = lambda b: b
    ctx_row = lambda b: B

    pos = _grid_pos_embed(L // GRID_W)
    lbs_f = _lower_bounds(hgrn_lb_fwd)
    lbs_b = _lower_bounds(hgrn_lb_bwd)
    bs_full = jnp.broadcast_to(gmlp_bs[..., None], gmlp_bs.shape + (GROUP_W,))
    ffn1 = [w.astype(BF16) for w in (ffn1_w1, ffn1_w3, ffn1_w2)]
    ffn2 = [w.astype(BF16) for w in (ffn2_w1, ffn2_w3, ffn2_w2)]
    zero_state = jnp.zeros((B, N_HEADS, GROUP_W, GROUP_W), F32)
    mix1 = functools.partial(_mix1, gn=norm_mix_g, w_in=w_in.astype(BF16), ws=gmlp_ws.astype(BF16),
                             bs=bs_full, gv=gmlp_norm_g, lbf=lbs_f, lbb=lbs_b)
    mix2 = functools.partial(_mix2, go=hgrn_norm_g, w_out=w_out.astype(BF16))

    def ctx_ffn(cx, g, w1, w3, w2, **kw):
        flat = cx.reshape(1, B * n_ctx, D_MODEL)
        return _ffn(flat, mods, ctx_row, g, w1, w3, w2, tm=tm_ctx_ffn, **kw).reshape(B, n_ctx, D_MODEL)

    lat, cx = x, ctx
    for l in range(DEPTH):
        last = l == DEPTH - 1
        lat = _ffn(lat, mods, lat_row, norm_ffn1_g, *ffn1, layer=l, j0=0, tm=tm_ffn,
                   pos=pos if l == 0 else None)
        cx = ctx_ffn(cx, norm_ffn1_g, *ffn1, layer=l, j0=0)

        *parts_c, s_f = mix1(cx, mods, ctx_row, s0=zero_state, layer=l, tm=tm_ctx)
        cx_mixed, s_b = mix2(cx, mods, ctx_row, *parts_c, s0=zero_state, layer=l, tm=tm_ctx)
        *parts_l, dg_l, _ = mix1(lat, mods, lat_row, s0=s_f, layer=l, tm=tm_lat)
        dg_l = dg_l.reshape(B, L // tm_mix2, tm_mix2 // GLA_CHUNK, B_W)
        lat, _ = mix2(lat, mods, lat_row, *parts_l, dg_l, s0=s_b, layer=l, tm=tm_mix2)

        lat = _ffn(lat, mods, lat_row, norm_ffn2_g, *ffn2, layer=l, j0=6, tm=tm_ffn,
                   gfin=norm_final_g if last else None)
        if not last:
            cx = ctx_ffn(cx_mixed, norm_ffn2_g, *ffn2, layer=l, j0=6)
    return lat
```
